```python
import functools
import jax
import jax.numpy as jnp
from jax import lax
import numpy as np

D_MODEL = 1024
BATCH = 2
SEQ = 8192
DEPTH = 4
DEC_BATCH = 32
DEC_SEQ = 8
PAST_LEN = 8192
PAGE_SIZE = 128

W_BR = 256
HA = 4
DH = 64
WA = HA * DH
Q_BLOCK = 128
ATTN_SCALE = DH ** -0.5
GB = 4
DB = 64
WB = GB * DB
CHUNK = 128
WC = W_BR
KC = 31
WD = W_BR
KD = 3
N_BRANCH = 4
N_GROUPS = 4
EXP_PER_GROUP = 4
N_EXPERTS = N_GROUPS * EXP_PER_GROUP
TOP_K = 2
D_EXPERT = 256
D_PLE = 256
LN_EPS = 1e-5
ALPHA = (2 * DEPTH) ** 0.25
BETA = (8 * DEPTH) ** -0.25

_SPLIT_SIZES = (WA, WA, WA, HA, WB, WB, 2 * WC, WD, WD, WD, N_BRANCH * D_MODEL)
D_IN = sum(_SPLIT_SIZES)
_SPLIT_POINTS = tuple(int(s) for s in np.cumsum(_SPLIT_SIZES)[:-1])

kernel_name = 'fox_gmlp_conformer_shortconv_hmoe_step'


def layer_norm(x, g, b):
    xf = x.astype(jnp.float32)
    mu = jnp.mean(xf, axis=-1, keepdims=True)
    var = jnp.mean(jnp.square(xf - mu), axis=-1, keepdims=True)
    return ((xf - mu) * lax.rsqrt(var + LN_EPS) * g + b).astype(x.dtype)


def depthwise_conv(xp, w):
    c = xp.shape[-1]
    return lax.conv_general_dilated(xp, w[:, None, :].astype(xp.dtype), window_strides=(1,), padding='VALID',
                                    dimension_numbers=('NWC', 'WIO', 'NWC'), feature_group_count=c)


def fox_block(q, pos_q, c_q, k, v, c_k, pos_k):
    s = jnp.einsum('bqhd,bkhd->bhqk', q, k, preferred_element_type=jnp.float32) * ATTN_SCALE
    s = s + jnp.swapaxes(c_q, 1, 2)[..., :, None] - jnp.swapaxes(c_k, 1, 2)[..., None, :]
    s = jnp.where(pos_k[None, :] <= pos_q[:, None], s, -jnp.inf)
    p = jax.nn.softmax(s, axis=-1)
    return jnp.einsum('bhqk,bkhd->bqhd', p.astype(v.dtype), v)


def fox_prompt(q, k, v, logf):
    b, s = q.shape[:2]
    nb = s // Q_BLOCK
    c = jnp.cumsum(logf, axis=1)
    pos_k = jnp.arange(s)
    qb = jnp.swapaxes(q.reshape(b, nb, Q_BLOCK, HA, DH), 0, 1)
    cb = jnp.swapaxes(c.reshape(b, nb, Q_BLOCK, HA), 0, 1)

    def one_block(args):
        i, q_i, c_i = args
        pos_q = i * Q_BLOCK + jnp.arange(Q_BLOCK)
        return fox_block(q_i, pos_q, c_i, k, v, c, pos_k)

    out = lax.map(one_block, (jnp.arange(nb), qb, cb))
    return jnp.swapaxes(out, 0, 1).reshape(b, s, HA, DH)


def fox_sample(q, k, v, logf, cache_k, cache_v, cache_logf, page_table, l):
    nb, n_pages = page_table.shape
    past = n_pages * PAGE_SIZE
    t = q.shape[1]
    k_all = jnp.concatenate([cache_k[l, page_table].reshape(nb, past, HA, DH), k], axis=1)
    v_all = jnp.concatenate([cache_v[l, page_table].reshape(nb, past, HA, DH), v], axis=1)
    lf_all = jnp.concatenate([cache_logf[l, page_table].reshape(nb, past, HA).astype(jnp.float32), logf], axis=1)
    c_all = jnp.cumsum(lf_all, axis=1)
    return fox_block(q, past + jnp.arange(t), c_all[:, past:], k_all, v_all, c_all, jnp.arange(past + t))


def spatial_gate(u, vn, w_s, b_s, n):
    b, t = u.shape[:2]
    w = jnp.tril(w_s)[:, :n, :n]
    vc = vn.reshape(b, t // n, n, GB, DB)
    mix = jnp.einsum('gij,bcjgd->bcigd', w, vc) + jnp.transpose(b_s[:, :n])[:, :, None]
    return u * mix.reshape(b, t, GB, DB)


def hier_moe(x, w_rg, b_rg, w_re, b_re, w_g, w_u, w_d):
    lg = jnp.matmul(x, w_rg, preferred_element_type=jnp.float32) + b_rg
    pg = jax.nn.softmax(lg, axis=-1)
    p_top, g_idx = lax.top_k(pg, 1)
    le = (jnp.matmul(x, w_re, preferred_element_type=jnp.float32) + b_re).reshape(*x.shape[:-1], N_GROUPS, EXP_PER_GROUP)
    g_onehot = jax.nn.one_hot(g_idx[..., 0], N_GROUPS, dtype=jnp.float32)
    le_sel = jnp.sum(le * g_onehot[..., None], axis=-2)
    v_top, e_idx = lax.top_k(le_sel, TOP_K)
    w_top = jax.nn.softmax(v_top, axis=-1) * p_top
    e_ids = g_idx * EXP_PER_GROUP + e_idx
    comb = jnp.sum(jax.nn.one_hot(e_ids, N_EXPERTS, dtype=jnp.float32) * w_top[..., None], axis=-2)
    hg = jnp.einsum('btd,edf->btef', x, w_g)
    hu = jnp.einsum('btd,edf->btef', x, w_u)
    hh = jax.nn.silu(hg) * hu * comb[..., None].astype(x.dtype)
    return jnp.einsum('btef,efd->btd', hh, w_d)


def setup_inputs(seed: int = 0) -> dict:
    key = jax.random.key(seed)
    ks = iter(jax.random.split(key, 48))

    def nrm(shape, scale):
        return jax.random.normal(next(ks), shape, jnp.float32) * scale

    n_pages = PAST_LEN // PAGE_SIZE
    n_used = DEC_BATCH * n_pages
    n_pool = n_used + (n_used + 3) // 4
    perm = jax.random.permutation(next(ks), n_pool)
    page_table = perm[:n_used].reshape(DEC_BATCH, n_pages).astype(jnp.int32)
    return {
        'x_prompt': nrm((BATCH, SEQ, D_MODEL), 1.0),
        'x_sample': nrm((DEC_BATCH, DEC_SEQ, D_MODEL), 1.0),
        'cache_k': nrm((DEPTH, n_pool, PAGE_SIZE, HA, DH), 1.0),
        'cache_v': nrm((DEPTH, n_pool, PAGE_SIZE, HA, DH), 1.0),
        'cache_logf': jax.nn.log_sigmoid(nrm((DEPTH, n_pool, PAGE_SIZE, HA), 1.0) + 2.0),
        'state_conv_c': nrm((DEPTH, DEC_BATCH, KC - 1, WC), 1.0),
        'state_conv_d': nrm((DEPTH, DEC_BATCH, KD - 1, WD), 1.0),
        'page_table': page_table,
        'p_prompt': nrm((DEPTH, BATCH, SEQ, D_PLE), 1.0),
        'p_sample': nrm((DEPTH, DEC_BATCH, DEC_SEQ, D_PLE), 1.0),
        'w_in': nrm((DEPTH, D_MODEL, D_IN), D_MODEL ** -0.5),
        'b_forget': 2.0 + nrm((DEPTH, HA), 0.1),
        'ln_gmlp_g': 1.0 + nrm((DEPTH, GB, DB), 0.02),
        'ln_gmlp_b': nrm((DEPTH, GB, DB), 0.02),
        'w_spatial': nrm((DEPTH, GB, CHUNK, CHUNK), CHUNK ** -0.5),
        'b_spatial': 1.0 + nrm((DEPTH, GB, CHUNK), 0.1),
        'w_conv_c': nrm((DEPTH, KC, WC), KC ** -0.5),
        'b_conv_c': nrm((DEPTH, WC), 0.02),
        'ln_conv_g': 1.0 + nrm((DEPTH, WC), 0.02),
        'ln_conv_b': nrm((DEPTH, WC), 0.02),
        'w_conv_d': nrm((DEPTH, KD, WD), KD ** -0.5),
        'w_branch': nrm((DEPTH, N_BRANCH, W_BR, D_MODEL), BETA * W_BR ** -0.5),
        'w_out': nrm((DEPTH, D_MODEL, D_MODEL), BETA * D_MODEL ** -0.5),
        'ln1_g': 1.0 + nrm((DEPTH, D_MODEL), 0.02),
        'ln1_b': nrm((DEPTH, D_MODEL), 0.02),
        'w_router_group': nrm((DEPTH, D_MODEL, N_GROUPS), D_MODEL ** -0.5),
        'b_router_group': nrm((DEPTH, N_GROUPS), 0.01),
        'w_router_expert': nrm((DEPTH, D_MODEL, N_EXPERTS), D_MODEL ** -0.5),
        'b_router_expert': nrm((DEPTH, N_EXPERTS), 0.01),
        'w_exp_gate': nrm((DEPTH, N_EXPERTS, D_MODEL, D_EXPERT), D_MODEL ** -0.5),
        'w_exp_up': nrm((DEPTH, N_EXPERTS, D_MODEL, D_EXPERT), D_MODEL ** -0.5),
        'w_exp_down': nrm((DEPTH, N_EXPERTS, D_EXPERT, D_MODEL), BETA * D_EXPERT ** -0.5),
        'w_ple_gate': nrm((DEPTH, D_MODEL, D_MODEL), D_MODEL ** -0.5),
        'w_ple_proj': nrm((DEPTH, D_PLE, D_MODEL), D_PLE ** -0.5),
        'ln2_g': 1.0 + nrm((DEPTH, D_MODEL), 0.02),
        'ln2_b': nrm((DEPTH, D_MODEL), 0.02),
    }


def reference(x_prompt, x_sample, cache_k, cache_v, cache_logf, state_conv_c, state_conv_d, page_table,
              p_prompt, p_sample, w_in, b_forget, ln_gmlp_g, ln_gmlp_b, w_spatial, b_spatial, w_conv_c,
              b_conv_c, ln_conv_g, ln_conv_b, w_conv_d, w_branch, w_out, ln1_g, ln1_b, w_router_group,
              b_router_group, w_router_expert, b_router_expert, w_exp_gate, w_exp_up, w_exp_down,
              w_ple_gate, w_ple_proj, ln2_g, ln2_b):

    def layer(l, x, ple, attend, buf_c, buf_d):
        nb, t, _ = x.shape
        z = x @ w_in[l]
        q, k, v, f, u_b, v_b, glu_c, b_d, c_d, h_d, g_br = jnp.split(z, _SPLIT_POINTS, axis=-1)
        q = q.reshape(nb, t, HA, DH)
        k = k.reshape(nb, t, HA, DH)
        v = v.reshape(nb, t, HA, DH)
        logf = jax.nn.log_sigmoid((f + b_forget[l]).astype(jnp.float32))
        y_a = attend(q, k, v, logf).reshape(nb, t, WA)
        u_b = jax.nn.gelu(u_b).reshape(nb, t, GB, DB)
        vn = layer_norm(jax.nn.gelu(v_b).reshape(nb, t, GB, DB), ln_gmlp_g[l], ln_gmlp_b[l])
        y_b = spatial_gate(u_b, vn, w_spatial[l], b_spatial[l], min(t, CHUNK)).reshape(nb, t, WB)
        a_c, g_c = jnp.split(glu_c, 2, axis=-1)
        xc = jnp.concatenate([buf_c, a_c * jax.nn.sigmoid(g_c)], axis=1)
        yc = depthwise_conv(xc, w_conv_c[l]) + b_conv_c[l]
        y_c = jax.nn.silu(layer_norm(yc, ln_conv_g[l], ln_conv_b[l]))
        xd = jnp.concatenate([buf_d, c_d * h_d], axis=1)
        y_d = b_d * depthwise_conv(xd, w_conv_d[l])
        ys = jnp.stack([y_a, y_b, y_c, y_d], axis=-2)
        proj = jnp.einsum('btnw,nwd->btnd', ys, w_branch[l])
        gates = jax.nn.sigmoid(g_br).reshape(nb, t, N_BRANCH, D_MODEL)
        mix = jnp.sum(gates * proj, axis=-2) @ w_out[l]
        x = layer_norm(ALPHA * x + mix, ln1_g[l], ln1_b[l])
        moe = hier_moe(x, w_router_group[l], b_router_group[l], w_router_expert[l], b_router_expert[l],
                       w_exp_gate[l], w_exp_up[l], w_exp_down[l])
        ple_term = jax.nn.sigmoid(x @ w_ple_gate[l]) * (ple @ w_ple_proj[l])
        x = layer_norm(ALPHA * x + moe + ple_term, ln2_g[l], ln2_b[l])
        return x, (k, v, logf, xc[:, -(KC - 1):], xd[:, -(KD - 1):], vn)

    xp, xs = x_prompt, x_sample
    zeros_c = jnp.zeros((x_prompt.shape[0], KC - 1, WC), x_prompt.dtype)
    zeros_d = jnp.zeros((x_prompt.shape[0], KD - 1, WD), x_prompt.dtype)
    kp, vp, lfp, ccp, cdp = [], [], [], [], []
    ks_, vs_, lfs, ccs, cds, gvs = [], [], [], [], [], []
    for l in range(DEPTH):
        xp, sp = layer(l, xp, p_prompt[l], fox_prompt, zeros_c, zeros_d)
        attend_s = functools.partial(fox_sample, cache_k=cache_k, cache_v=cache_v, cache_logf=cache_logf,
                                     page_table=page_table, l=l)
        xs, ss = layer(l, xs, p_sample[l], attend_s, state_conv_c[l], state_conv_d[l])
        kp.append(sp[0]); vp.append(sp[1]); lfp.append(sp[2]); ccp.append(sp[3]); cdp.append(sp[4])
        ks_.append(ss[0]); vs_.append(ss[1]); lfs.append(ss[2]); ccs.append(ss[3]); cds.append(ss[4]); gvs.append(ss[5])
    k_prompt = jnp.stack(kp)
    v_prompt = jnp.stack(vp)
    logf_prompt = jnp.stack(lfp)
    k_sample = jnp.stack(ks_)
    v_sample = jnp.stack(vs_)
    logf_sample = jnp.stack(lfs)
    conv_c_prompt = jnp.stack(ccp)
    conv_c_sample = jnp.stack(ccs)
    conv_d_prompt = jnp.stack(cdp)
    conv_d_sample = jnp.stack(cds)
    gmlp_v_sample = jnp.stack(gvs)
    return (xp, xs, k_prompt, v_prompt, logf_prompt, k_sample, v_sample, logf_sample,
            conv_c_prompt, conv_c_sample, conv_d_prompt, conv_d_sample, gmlp_v_sample)
```

```python
import functools

import numpy as np
import jax
import jax.numpy as jnp
from jax import lax
from jax.experimental import pallas as pl
from jax.experimental.pallas import tpu as pltpu

F32 = jnp.float32
BF16 = jnp.bfloat16

D_MODEL = 1024
W_BR = 256
HA = 4
DH = 64
GB = 4
DB = 64
CHUNK = 128
KC = 31
KD = 3
N_BRANCH = 4
N_GROUPS = 4
EXP_PER_GROUP = 4
N_EXPERTS = 16
D_EXPERT = 256
D_PLE = 256
PAGE = 128
LN_EPS = 1e-5
ATTN_SCALE = DH ** -0.5
LANES = 128
SUBLANES = 8
HALO_C = 32
HALO_D = 8
W_MIX = 10 * W_BR + LANES
VMEM_LIMIT = 56 * 1024 * 1024


def _cparams(n_axes):
    return pltpu.CompilerParams(dimension_semantics=("arbitrary",) * n_axes,
                                vmem_limit_bytes=VMEM_LIMIT)


def _const_spec(shape, index_map):
    return pl.BlockSpec(shape, index_map, pipeline_mode=pl.Buffered(1))


def _dot(a, b):
    return jnp.dot(a, b, preferred_element_type=F32)


def _dot_nt(a, b):
    return lax.dot_general(a, b, (((1,), (1,)), ((), ())), preferred_element_type=F32)


def _split3(x):
    hi = x.astype(BF16)
    r = x - hi.astype(F32)
    mid = r.astype(BF16)
    lo = (r - mid.astype(F32)).astype(BF16)
    return hi, mid, lo


def _dot3_r(x, m):
    hi, mid, lo = _split3(x)
    return _dot(hi, m) + _dot(mid, m) + _dot(lo, m)


def _dot3_l(m, x):
    hi, mid, lo = _split3(x)
    return _dot(m, hi) + _dot(m, mid) + _dot(m, lo)


def _dot3_nt(m, x):
    hi, mid, lo = _split3(x)
    return _dot_nt(m, hi) + _dot_nt(m, mid) + _dot_nt(m, lo)


def _sigmoid(x):
    return 1.0 / (1.0 + jnp.exp(-x))


def _silu(x):
    return x * _sigmoid(x)


def _gelu(x):
    return 0.5 * x * (1.0 + jnp.tanh(0.7978845608028654 * (x + 0.044715 * (x * x * x))))


def _log_sigmoid(x):
    return jnp.minimum(x, 0.0) - jnp.log1p(jnp.exp(-jnp.abs(x)))


def _layer_norm(x, g, b):
    mu = jnp.mean(x, axis=-1, keepdims=True)
    d = x - mu
    var = jnp.mean(d * d, axis=-1, keepdims=True)
    return d * lax.rsqrt(var + LN_EPS) * g + b


def _lane_group_mask(rows, width, group, g):
    lane = lax.broadcasted_iota(jnp.int32, (rows, width), 1)
    return (lane >= g * group) & (lane < (g + 1) * group)


def _mixer_front(x, wmix_ref, bf_ref):
    z = _dot(x.astype(BF16), wmix_ref[...])
    seg = lambda i: z[:, i * W_BR:(i + 1) * W_BR]
    f = z[:, 10 * W_BR:10 * W_BR + LANES]
    lane = lax.broadcasted_iota(jnp.int32, f.shape, 1)
    logf = jnp.where(lane < HA, _log_sigmoid(f + bf_ref[...]), 0.0)
    return [seg(i) for i in range(10)], logf


def _gmlp(ub, vb, gavg_ref, lng_ref, lnb_ref, msp_ref, bsp_ref):
    rows = ub.shape[0]
    ug = _gelu(ub)
    vg = _gelu(vb)
    gavg = gavg_ref[...]
    mu = _dot3_r(vg, gavg)
    dv = vg - mu
    var = _dot3_r(dv * dv, gavg)
    vn = dv * lax.rsqrt(var + LN_EPS) * lng_ref[...] + lnb_ref[...]
    vnb = vn.astype(BF16)
    mix = bsp_ref[...]
    for g in range(GB):
        mg = _dot(msp_ref[g], vnb)
        mix = mix + jnp.where(_lane_group_mask(rows, W_BR, DB, g), mg, 0.0)
    return ug * mix, vn


def _k1_prompt_kernel(x_ref, wmix_ref, bf_ref, gavg_ref, lng_ref, lnb_ref, msp_ref, bsp_ref,
                      wc_ref, bc_ref, lcg_ref, lcb_ref, wd_ref, tril_ref,
                      q_ref, k_ref, v_ref, kb_ref, vb_ref, logf_ref, ct_ref,
                      yb_ref, yc_ref, yd_ref, sc_ref, sd_ref,
                      xpc, xpd, carry, *, tm):
    i = pl.program_id(1)

    @pl.when(i == 0)
    def _():
        xpc[0:HALO_C, :] = jnp.zeros((HALO_C, W_BR), F32)
        xpd[0:HALO_D, :] = jnp.zeros((HALO_D, W_BR), F32)
        carry[...] = jnp.zeros_like(carry)

    (q, k, v, ub, vb, ac, gc, bd, cd, hd), logf = _mixer_front(x_ref[0], wmix_ref, bf_ref)
    q_ref[0] = (q * ATTN_SCALE).astype(BF16)
    k_ref[0] = k
    v_ref[0] = v
    kb_ref[0] = k.astype(BF16)
    vb_ref[0] = v.astype(BF16)
    logf_ref[0] = logf[:, 0:HA]

    c = _dot3_l(tril_ref[...], logf) + carry[0:1, :]
    carry[0:1, :] = c[tm - 1:tm, :]
    ct_ref[0] = c.T[0:SUBLANES, :]

    yb, _ = _gmlp(ub, vb, gavg_ref, lng_ref, lnb_ref, msp_ref, bsp_ref)
    yb_ref[0] = yb.astype(BF16)

    cin = ac * _sigmoid(gc)
    xpc[HALO_C:HALO_C + tm, :] = cin
    acc = jnp.zeros((tm, W_BR), F32) + bc_ref[...]
    for j in range(KC):
        off = HALO_C - (KC - 1) + j
        acc = acc + wc_ref[j:j + 1, :] * xpc[off:off + tm, :]
    xpc[0:HALO_C, :] = cin[tm - HALO_C:tm, :]
    sc_ref[0] = cin[tm - HALO_C:tm, :]
    yc_ref[0] = _silu(_layer_norm(acc, lcg_ref[...], lcb_ref[...])).astype(BF16)

    din = cd * hd
    xpd[HALO_D:HALO_D + tm, :] = din
    accd = jnp.zeros((tm, W_BR), F32)
    for j in range(KD):
        off = HALO_D - (KD - 1) + j
        accd = accd + wd_ref[j:j + 1, :] * xpd[off:off + tm, :]
    xpd[0:HALO_D, :] = din[tm - HALO_D:tm, :]
    sd_ref[0] = din[tm - HALO_D:tm, :]
    yd_ref[0] = (bd * accd).astype(BF16)


def _k1_prompt(l, x, wp, tm):
    b, s, _ = x.shape
    nt = s // tm
    row = lambda bb, i: (bb, i, 0)
    cst2 = lambda bb, i: (l, 0, 0)
    o256 = lambda dt: jax.ShapeDtypeStruct((b, s, W_BR), dt)
    in_specs = [
        pl.BlockSpec((1, tm, D_MODEL), row),
        _const_spec((None, D_MODEL, W_MIX), cst2),
        _const_spec((None, 1, LANES), cst2),
        _const_spec((W_BR, W_BR), lambda bb, i: (0, 0)),
        _const_spec((None, 1, W_BR), cst2),
        _const_spec((None, 1, W_BR), cst2),
        _const_spec((None, GB, tm, tm), lambda bb, i: (l, 0, 0, 0)),
        _const_spec((None, tm, W_BR), cst2),
        _const_spec((None, HALO_C, W_BR), cst2),
        _const_spec((None, 1, W_BR), cst2),
        _const_spec((None, 1, W_BR), cst2),
        _const_spec((None, 1, W_BR), cst2),
        _const_spec((None, HALO_D, W_BR), cst2),
        _const_spec((tm, tm), lambda bb, i: (0, 0)),
    ]
    out_shape = [o256(BF16), o256(F32), o256(F32), o256(BF16), o256(BF16),
                 jax.ShapeDtypeStruct((b, s, HA), F32),
                 jax.ShapeDtypeStruct((b, SUBLANES, s), F32),
                 o256(BF16), o256(BF16), o256(BF16),
                 jax.ShapeDtypeStruct((b, HALO_C, W_BR), F32),
                 jax.ShapeDtypeStruct((b, HALO_D, W_BR), F32)]
    blk = pl.BlockSpec((1, tm, W_BR), row)
    out_specs = [blk, blk, blk, blk, blk,
                 pl.BlockSpec((1, tm, HA), row),
                 pl.BlockSpec((1, SUBLANES, tm), lambda bb, i: (bb, 0, i)),
                 blk, blk, blk,
                 pl.BlockSpec((1, HALO_C, W_BR), lambda bb, i: (bb, 0, 0)),
                 pl.BlockSpec((1, HALO_D, W_BR), lambda bb, i: (bb, 0, 0))]
    return pl.pallas_call(
        functools.partial(_k1_prompt_kernel, tm=tm),
        grid=(b, nt), in_specs=in_specs, out_specs=out_specs, out_shape=out_shape,
        scratch_shapes=[pltpu.VMEM((tm + HALO_C, W_BR), F32), pltpu.VMEM((tm + HALO_D, W_BR), F32),
                        pltpu.VMEM((SUBLANES, LANES), F32)],
        compiler_params=_cparams(2), name="k1_prompt",
    )(x, wp["wmix"], wp["bf"], wp["gavg"], wp["lng"], wp["lnb"], wp["msp_p"], wp["bsp_p"],
      wp["wc"], wp["bc"], wp["lcg"], wp["lcb"], wp["wd"], wp["tril_p"])


def _k1_sample_kernel(x_ref, wmix_ref, bf_ref, gavg_ref, lng_ref, lnb_ref, msp_ref, bsp_ref,
                      wc_ref, bc_ref, lcg_ref, lcb_ref, wd_ref, stc_ref, std_ref,
                      q_ref, k_ref, v_ref, logf_ref, vn_ref, yb_ref, yc_ref, yd_ref, nsc_ref, nsd_ref,
                      cin_s, ypre_s, din_s, ydpre_s, xs, xsd, *, nb, t):
    (q, k, v, ub, vb, ac, gc, bd, cd, hd), logf = _mixer_front(x_ref[...], wmix_ref, bf_ref)
    q_ref[...] = q * ATTN_SCALE
    k_ref[...] = k
    v_ref[...] = v
    logf_ref[...] = logf
    yb, vn = _gmlp(ub, vb, gavg_ref, lng_ref, lnb_ref, msp_ref, bsp_ref)
    yb_ref[...] = yb.astype(BF16)
    vn_ref[...] = vn

    cin_s[...] = ac * _sigmoid(gc)
    din_s[...] = cd * hd

    def body(bb, _):
        r0 = pl.multiple_of(bb * t, t)
        xs[0:KC - 1, :] = stc_ref[bb]
        xs[KC - 1:KC - 1 + t, :] = cin_s[pl.ds(r0, t), :]
        acc = jnp.zeros((t, W_BR), F32) + bc_ref[...]
        for j in range(KC):
            acc = acc + wc_ref[j:j + 1, :] * xs[j:j + t, :]
        ypre_s[pl.ds(r0, t), :] = acc
        nsc_ref[bb] = xs[t:t + KC - 1, :]

        xsd[0:KD - 1, :] = std_ref[bb]
        xsd[KD - 1:KD - 1 + t, :] = din_s[pl.ds(r0, t), :]
        accd = jnp.zeros((t, W_BR), F32)
        for j in range(KD):
            accd = accd + wd_ref[j:j + 1, :] * xsd[j:j + t, :]
        ydpre_s[pl.ds(r0, t), :] = accd
        nsd_ref[bb] = xsd[t:t + KD - 1, :]
        return 0

    lax.fori_loop(0, nb, body, 0)
    yc_ref[...] = _silu(_layer_norm(ypre_s[...], lcg_ref[...], lcb_ref[...])).astype(BF16)
    yd_ref[...] = (bd * ydpre_s[...]).astype(BF16)


def _k1_sample(l, x, stc, std, wp, nb, t):
    m = nb * t
    cst2 = lambda i: (l, 0, 0)
    cst4 = lambda i: (l, 0, 0, 0)
    full = lambda shp: pl.BlockSpec(shp, lambda i: (0,) * len(shp))
    in_specs = [
        full((m, D_MODEL)),
        pl.BlockSpec((None, D_MODEL, W_MIX), cst2),
        pl.BlockSpec((None, 1, LANES), cst2),
        full((W_BR, W_BR)),
        pl.BlockSpec((None, 1, W_BR), cst2),
        pl.BlockSpec((None, 1, W_BR), cst2),
        pl.BlockSpec((None, GB, m, m), cst4),
        pl.BlockSpec((None, m, W_BR), cst2),
        pl.BlockSpec((None, HALO_C, W_BR), cst2),
        pl.BlockSpec((None, 1, W_BR), cst2),
        pl.BlockSpec((None, 1, W_BR), cst2),
        pl.BlockSpec((None, 1, W_BR), cst2),
        pl.BlockSpec((None, HALO_D, W_BR), cst2),
        pl.BlockSpec((None, nb, KC - 1, W_BR), cst4),
        pl.BlockSpec((None, nb, KD - 1, W_BR), cst4),
    ]
    o = lambda dt: jax.ShapeDtypeStruct((m, W_BR), dt)
    out_shape = [o(F32), o(F32), o(F32), jax.ShapeDtypeStruct((m, LANES), F32), o(F32),
                 o(BF16), o(BF16), o(BF16),
                 jax.ShapeDtypeStruct((nb, KC - 1, W_BR), F32),
                 jax.ShapeDtypeStruct((nb, KD - 1, W_BR), F32)]
    out_specs = [full(s.shape) for s in out_shape]
    return pl.pallas_call(
        functools.partial(_k1_sample_kernel, nb=nb, t=t),
        grid=(1,), in_specs=in_specs, out_specs=out_specs, out_shape=out_shape,
        scratch_shapes=[pltpu.VMEM((m, W_BR), F32), pltpu.VMEM((m, W_BR), F32),
                        pltpu.VMEM((m, W_BR), F32), pltpu.VMEM((m, W_BR), F32),
                        pltpu.VMEM((KC - 1 + t + 2, W_BR), F32), pltpu.VMEM((2 * SUBLANES, W_BR), F32)],
        compiler_params=pltpu.CompilerParams(vmem_limit_bytes=VMEM_LIMIT), name="k1_sample",
    )(x, wp["wmix"], wp["bf"], wp["gavg"], wp["lng"], wp["lnb"], wp["msp_s"], wp["bsp_s"],
      wp["wc"], wp["bc"], wp["lcg"], wp["lcb"], wp["wd"], stc, std)


def _attn_prompt_kernel(qi_tab, ki_tab, q_ref, k_ref, v_ref, ck_ref, cq_ref, o_ref,
                        qm_s, m_s, l_s, acc_s, *, tq):
    p = pl.program_id(1)
    qi = qi_tab[p]
    ki = ki_tab[p]

    @pl.when(ki == 0)
    def _():
        q = q_ref[0]
        for h in range(HA):
            qm_s[h] = jnp.where(_lane_group_mask(tq, W_BR, DH, h), q, jnp.zeros_like(q))
        m_s[...] = jnp.full(m_s.shape, -jnp.inf, F32)
        l_s[...] = jnp.zeros(l_s.shape, F32)
        acc_s[...] = jnp.zeros(acc_s.shape, F32)

    def step(masked):
        k = k_ref[0]
        v = v_ref[0]
        ck = ck_ref[0]
        cq = cq_ref[0]
        if masked:
            row = lax.broadcasted_iota(jnp.int32, (tq, tq), 0)
            col = lax.broadcasted_iota(jnp.int32, (tq, tq), 1)
            keep = col <= row
        for h in range(HA):
            bias = cq[h:h + 1, 0:1] - ck[h:h + 1, :]
            s = _dot_nt(qm_s[h], k) + bias
            if masked:
                s = jnp.where(keep, s, -jnp.inf)
            m_prev = m_s[h]
            m_new = jnp.maximum(m_prev, jnp.max(s, axis=1, keepdims=True))
            alpha = jnp.exp(m_prev - m_new)
            pr = jnp.exp(s - m_new)
            l_s[h] = alpha * l_s[h] + jnp.sum(pr, axis=1, keepdims=True)
            acc_s[h] = acc_s[h] * alpha + _dot(pr.astype(BF16), v)
            m_s[h] = m_new

    @pl.when(ki < qi)
    def _():
        step(False)

    @pl.when(ki == qi)
    def _():
        step(True)
        out = jnp.zeros((tq, W_BR), F32)
        for h in range(HA):
            out = out + jnp.where(_lane_group_mask(tq, W_BR, DH, h), acc_s[h] / l_s[h], 0.0)
        o_ref[0] = out.astype(BF16)


def _attn_prompt(q, k, v, ct, tq):
    b, s, _ = q.shape
    nq = s // tq
    qi_np = np.concatenate([np.full(i + 1, i, np.int32) for i in range(nq)])
    ki_np = np.concatenate([np.arange(i + 1, dtype=np.int32) for i in range(nq)])
    n_pairs = int(qi_np.shape[0])
    grid_spec = pltpu.PrefetchScalarGridSpec(
        num_scalar_prefetch=2, grid=(b, n_pairs),
        in_specs=[
            pl.BlockSpec((1, tq, W_BR), lambda bb, p, qt, kt: (bb, qt[p], 0)),
            pl.BlockSpec((1, tq, W_BR), lambda bb, p, qt, kt: (bb, kt[p], 0)),
            pl.BlockSpec((1, tq, W_BR), lambda bb, p, qt, kt: (bb, kt[p], 0)),
            pl.BlockSpec((1, SUBLANES, tq), lambda bb, p, qt, kt: (bb, 0, kt[p])),
            pl.BlockSpec((1, SUBLANES, tq), lambda bb, p, qt, kt: (bb, 0, qt[p])),
        ],
        out_specs=pl.BlockSpec((1, tq, W_BR), lambda bb, p, qt, kt: (bb, qt[p], 0)),
        scratch_shapes=[pltpu.VMEM((HA, tq, W_BR), BF16), pltpu.VMEM((HA, tq, 1), F32),
                        pltpu.VMEM((HA, tq, 1), F32), pltpu.VMEM((HA, tq, W_BR), F32)],
    )
    return pl.pallas_call(
        functools.partial(_attn_prompt_kernel, tq=tq),
        grid_spec=grid_spec, out_shape=jax.ShapeDtypeStruct((b, s, W_BR), BF16),
        compiler_params=_cparams(2), name="attn_prompt",
    )(jnp.asarray(qi_np), jnp.asarray(ki_np), q, k, v, ct, ct)


def _page_cumsum_kernel(lf_ref, t_ref, o_ref):
    res = _dot3_r(lf_ref[...], t_ref[...])
    for s in range(SUBLANES):
        o_ref[:, s, :] = res[:, s * LANES:(s + 1) * LANES]


def _page_cumsum(lf2d, tmat, tr):
    n = lf2d.shape[0]
    return pl.pallas_call(
        _page_cumsum_kernel, grid=(n // tr,),
        in_specs=[pl.BlockSpec((tr, PAGE * HA), lambda i: (i, 0)),
                  _const_spec((PAGE * HA, SUBLANES * LANES), lambda i: (0, 0))],
        out_specs=pl.BlockSpec((tr, SUBLANES, LANES), lambda i: (i, 0, 0)),
        out_shape=jax.ShapeDtypeStruct((n, SUBLANES, LANES), F32),
        compiler_params=_cparams(1), name="page_cumsum",
    )(lf2d, tmat)


def _attn_sample_kernel(pt_ref, q_ref, kn_ref, vn_ref, lfn_ref, pmat_ref, a8_ref, ltri_ref, rsel_ref,
                        ck_hbm, cv_hbm, cc_hbm, o_ref, kbuf, vbuf, cbuf, sem, *, l, nb, t, n_pages):
    b = pl.program_id(0)
    slot = b % 2
    past = n_pages * PAGE
    rows = HA * t

    def copies(bb, sl, p):
        pg = pt_ref[bb, p]
        return (pltpu.make_async_copy(ck_hbm.at[l, pg], kbuf.at[sl, p], sem.at[sl, 0]),
                pltpu.make_async_copy(cv_hbm.at[l, pg], vbuf.at[sl, p], sem.at[sl, 1]),
                pltpu.make_async_copy(cc_hbm.at[l, pg], cbuf.at[sl, p], sem.at[sl, 2]))

    def issue(bb, sl):
        def body(p, _):
            for cp in copies(bb, sl, p):
                cp.start()
            return 0
        lax.fori_loop(0, n_pages, body, 0)

    @pl.when(b == 0)
    def _():
        issue(0, 0)

    @pl.when(b + 1 < nb)
    def _():
        issue(b + 1, 1 - slot)

    def wait_body(p, _):
        for cp in copies(b, slot, p):
            cp.wait()
        return 0
    lax.fori_loop(0, n_pages, wait_body, 0)

    q = q_ref[0]
    sub = lax.broadcasted_iota(jnp.int32, (SUBLANES, W_BR), 0)
    lane = lax.broadcasted_iota(jnp.int32, (SUBLANES, W_BR), 1)
    head_ok = (lane // DH) == (sub % HA)
    qm = []
    for g in range(t // 2):
        qa = jnp.broadcast_to(q[2 * g:2 * g + 1, :], (SUBLANES, W_BR))
        qb = jnp.broadcast_to(q[2 * g + 1:2 * g + 2, :], (SUBLANES, W_BR))
        qm.append(jnp.where(head_ok, jnp.where(sub < HA, qa, qb), 0.0))
    qm = jnp.concatenate(qm, axis=0).astype(BF16)

    c2 = cbuf[slot].reshape(n_pages * SUBLANES, LANES)
    off = _dot3_l(pmat_ref[...], c2)
    cf = c2 + off[:, LANES - 1:LANES]
    tot = cf[(n_pages - 1) * SUBLANES:n_pages * SUBLANES, LANES - 1:LANES]
    c_past = jnp.concatenate([cf[j * SUBLANES:(j + 1) * SUBLANES, :] for j in range(n_pages)], axis=1)
    lfp = jnp.concatenate([lfn_ref[0], jnp.zeros((LANES - t, LANES), F32)], axis=0)
    cum_new = _dot3_l(ltri_ref[...], lfp)
    c_new = _dot3_nt(a8_ref[...], cum_new) + tot

    k2 = kbuf[slot].reshape(past, W_BR)
    v2 = vbuf[slot].reshape(past, W_BR)
    s_past = _dot_nt(qm, k2).reshape(rows // SUBLANES, SUBLANES, past) - c_past[None]
    zpad = jnp.zeros((LANES - t, W_BR), F32)
    kn = jnp.concatenate([kn_ref[0], zpad], axis=0).astype(BF16)
    vn = jnp.concatenate([vn_ref[0], zpad], axis=0).astype(BF16)
    s_new = _dot_nt(qm, kn).reshape(rows // SUBLANES, SUBLANES, LANES) - c_new[None]
    gi = lax.broadcasted_iota(jnp.int32, s_new.shape, 0)
    si = lax.broadcasted_iota(jnp.int32, s_new.shape, 1)
    ji = lax.broadcasted_iota(jnp.int32, s_new.shape, 2)
    s_new = jnp.where(ji <= 2 * gi + si // HA, s_new, -jnp.inf)

    m = jnp.maximum(jnp.max(s_past, axis=2, keepdims=True), jnp.max(s_new, axis=2, keepdims=True))
    p_past = jnp.exp(s_past - m)
    p_new = jnp.exp(s_new - m)
    den = jnp.sum(p_past, axis=2, keepdims=True) + jnp.sum(p_new, axis=2, keepdims=True)
    o = (_dot(p_past.reshape(rows, past).astype(BF16), v2)
         + _dot(p_new.reshape(rows, LANES).astype(BF16), vn))
    o = o / den.reshape(rows, 1)
    sub_r = lax.broadcasted_iota(jnp.int32, (rows, W_BR), 0)
    lane_r = lax.broadcasted_iota(jnp.int32, (rows, W_BR), 1)
    om = jnp.where((lane_r // DH) == (sub_r % HA), o, 0.0)
    o_ref[0] = _dot(rsel_ref[...], om.astype(BF16))


def _attn_sample(l, page_table, q, kn, vn, lfn, wp, ck, cv, cc):
    nb, t, _ = q.shape
    n_pages = page_table.shape[1]
    rows = HA * t
    cst = lambda shp: pl.BlockSpec(shp, lambda b, pt: (0,) * len(shp))
    per_b = lambda w: pl.BlockSpec((1, t, w), lambda b, pt: (b, 0, 0))
    grid_spec = pltpu.PrefetchScalarGridSpec(
        num_scalar_prefetch=1, grid=(nb,),
        in_specs=[per_b(W_BR), per_b(W_BR), per_b(W_BR), per_b(LANES),
                  cst((n_pages * SUBLANES, n_pages * SUBLANES)), cst((SUBLANES, LANES)),
                  cst((LANES, LANES)), cst((t, rows)),
                  pl.BlockSpec(memory_space=pl.ANY), pl.BlockSpec(memory_space=pl.ANY),
                  pl.BlockSpec(memory_space=pl.ANY)],
        out_specs=per_b(W_BR),
        scratch_shapes=[pltpu.VMEM((2, n_pages, PAGE, W_BR), BF16),
                        pltpu.VMEM((2, n_pages, PAGE, W_BR), BF16),
                        pltpu.VMEM((2, n_pages, SUBLANES, LANES), F32),
                        pltpu.SemaphoreType.DMA((2, 3))],
    )
    return pl.pallas_call(
        functools.partial(_attn_sample_kernel, l=l, nb=nb, t=t, n_pages=n_pages),
        grid_spec=grid_spec, out_shape=jax.ShapeDtypeStruct((nb, t, W_BR), F32),
        compiler_params=_cparams(1), name="attn_sample",
    )(page_table, q, kn, vn, lfn, wp["pmat"], wp["a8"], wp["ltri"], wp["rsel"], ck, cv, cc)


def _merge_kernel(x_ref, ya_ref, yb_ref, yc_ref, yd_ref, wg_ref, wbr_ref, wo_ref, g_ref, b_ref, o_ref,
                  *, alpha):
    x = x_ref[...]
    xb = x.astype(BF16)
    ys = (ya_ref, yb_ref, yc_ref, yd_ref)
    acc = jnp.zeros(x.shape, F32)
    for n in range(N_BRANCH):
        gate = _sigmoid(_dot(xb, wg_ref[:, n * D_MODEL:(n + 1) * D_MODEL]))
        acc = acc + gate * _dot(ys[n][...].astype(BF16), wbr_ref[n])
    mix = _dot(acc.astype(BF16), wo_ref[...])
    o_ref[...] = _layer_norm(alpha * x + mix, g_ref[...], b_ref[...])


def _merge(l, x, ya, yb, yc, yd, wp, tm, alpha):
    m = x.shape[0]
    row = lambda w: pl.BlockSpec((tm, w), lambda i: (i, 0))
    c3 = lambda i: (l, 0, 0)
    return pl.pallas_call(
        functools.partial(_merge_kernel, alpha=alpha),
        grid=(m // tm,),
        in_specs=[row(D_MODEL), row(W_BR), row(W_BR), row(W_BR), row(W_BR),
                  _const_spec((None, D_MODEL, N_BRANCH * D_MODEL), c3),
                  _const_spec((None, N_BRANCH, W_BR, D_MODEL), lambda i: (l, 0, 0, 0)),
                  _const_spec((None, D_MODEL, D_MODEL), c3),
                  _const_spec((None, 1, D_MODEL), c3), _const_spec((None, 1, D_MODEL), c3)],
        out_specs=row(D_MODEL), out_shape=jax.ShapeDtypeStruct((m, D_MODEL), F32),
        compiler_params=_cparams(1), name="merge",
    )(x, ya, yb, yc, yd, wp["wgate"], wp["wbr"], wp["wout"], wp["ln1g"], wp["ln1b"])


def _moe_kernel(x_ref, ple_ref, wrh_ref, wrl_ref, br_ref, wg_ref, wu_ref, wd_ref, wpg_ref, wpp_ref,
                g_ref, b_ref, o_ref, *, alpha):
    x = x_ref[...]
    tm = x.shape[0]
    xb = x.astype(BF16)
    xm = (x - xb.astype(F32)).astype(BF16)

    logits = _dot(xb, wrh_ref[...]) + _dot(xb, wrl_ref[...]) + _dot(xm, wrh_ref[...]) + br_ref[...]
    lane = lax.broadcasted_iota(jnp.int32, (tm, LANES), 1)
    big = jnp.int32(LANES)
    lg = jnp.where(lane < N_GROUPS, logits, -jnp.inf)
    mg = jnp.max(lg, axis=1, keepdims=True)
    p_top = 1.0 / jnp.sum(jnp.exp(lg - mg), axis=1, keepdims=True)
    g_idx = jnp.min(jnp.where(lg == mg, lane, big), axis=1, keepdims=True)
    lo = N_GROUPS + g_idx * EXP_PER_GROUP
    le = jnp.where((lane >= lo) & (lane < lo + EXP_PER_GROUP), logits, -jnp.inf)
    v1 = jnp.max(le, axis=1, keepdims=True)
    i1 = jnp.min(jnp.where(le == v1, lane, big), axis=1, keepdims=True)
    le2 = jnp.where(lane == i1, -jnp.inf, le)
    v2 = jnp.max(le2, axis=1, keepdims=True)
    i2 = jnp.min(jnp.where(le2 == v2, lane, big), axis=1, keepdims=True)
    e2 = jnp.exp(v2 - v1)
    w1 = p_top / (1.0 + e2)
    w2 = p_top * e2 / (1.0 + e2)
    comb = jnp.where(lane == i1, w1, 0.0) + jnp.where(lane == i2, w2, 0.0)

    acc = jnp.zeros(x.shape, F32)
    for e in range(N_EXPERTS):
        hg = _dot(xb, wg_ref[e])
        hu = _dot(xb, wu_ref[e])
        hh = _silu(hg) * hu * comb[:, N_GROUPS + e:N_GROUPS + e + 1]
        acc = acc + _dot(hh.astype(BF16), wd_ref[e])

    ple_term = _sigmoid(_dot(xb, wpg_ref[...])) * _dot(ple_ref[...].astype(BF16), wpp_ref[...])
    o_ref[...] = _layer_norm(alpha * x + acc + ple_term, g_ref[...], b_ref[...])


def _moe(l, x, ple, wp, tm, alpha):
    m = x.shape[0]
    c3 = lambda i: (l, 0, 0)
    c4 = lambda i: (l, 0, 0, 0)
    return pl.pallas_call(
        functools.partial(_moe_kernel, alpha=alpha),
        grid=(m // tm,),
        in_specs=[pl.BlockSpec((tm, D_MODEL), lambda i: (i, 0)),
                  pl.BlockSpec((None, tm, D_PLE), lambda i: (l, i, 0)),
                  _const_spec((None, D_MODEL, LANES), c3), _const_spec((None, D_MODEL, LANES), c3),
                  _const_spec((None, 1, LANES), c3),
                  _const_spec((None, N_EXPERTS, D_MODEL, D_EXPERT), c4),
                  _const_spec((None, N_EXPERTS, D_MODEL, D_EXPERT), c4),
                  _const_spec((None, N_EXPERTS, D_EXPERT, D_MODEL), c4),
                  _const_spec((None, D_MODEL, D_MODEL), c3), _const_spec((None, D_PLE, D_MODEL), c3),
                  _const_spec((None, 1, D_MODEL), c3), _const_spec((None, 1, D_MODEL), c3)],
        out_specs=pl.BlockSpec((tm, D_MODEL), lambda i: (i, 0)),
        out_shape=jax.ShapeDtypeStruct((m, D_MODEL), F32),
        compiler_params=_cparams(1), name="moe",
    )(x, ple, wp["wr_hi"], wp["wr_lo"], wp["br"], wp["weg"], wp["weu"], wp["wed"],
      wp["wpg"], wp["wpp"], wp["ln2g"], wp["ln2b"])


def _kron_tril(w_spatial, n_blocks, n):
    w = jnp.tril(w_spatial)[:, :, :n, :n]
    eye = jnp.eye(n_blocks, dtype=w.dtype)
    out = jnp.einsum("ab,lgij->lgaibj", eye, w)
    d = w_spatial.shape[0]
    return out.reshape(d, GB, n_blocks * n, n_blocks * n).astype(BF16)


def _prepare(w_in, b_forget, ln_gmlp_g, ln_gmlp_b, w_spatial, b_spatial, w_conv_c, b_conv_c, ln_conv_g,
             ln_conv_b, w_conv_d, w_branch, w_out, ln1_g, ln1_b, w_router_group, b_router_group,
             w_router_expert, b_router_expert, w_exp_gate, w_exp_up, w_exp_down, w_ple_gate, w_ple_proj,
             ln2_g, ln2_b, tm_p, nb, t, n_pages):
    d = w_in.shape[0]
    o_f = 3 * W_BR
    o_rest = o_f + HA
    o_gate = o_rest + 7 * W_BR
    r3 = lambda a: a.reshape(d, 1, -1)
    wp = {}
    wp["wmix"] = jnp.concatenate(
        [w_in[:, :, 0:o_f], w_in[:, :, o_rest:o_gate],
         jnp.pad(w_in[:, :, o_f:o_rest], ((0, 0), (0, 0), (0, LANES - HA)))], axis=-1).astype(BF16)
    wp["wgate"] = w_in[:, :, o_gate:].astype(BF16)
    wp["bf"] = jnp.pad(b_forget, ((0, 0), (0, LANES - HA))).reshape(d, 1, LANES)
    gi = np.arange(W_BR) // DB
    wp["gavg"] = jnp.asarray((gi[:, None] == gi[None, :]).astype(np.float32) / DB, dtype=BF16)
    wp["lng"] = r3(ln_gmlp_g)
    wp["lnb"] = r3(ln_gmlp_b)
    wp["msp_p"] = _kron_tril(w_spatial, tm_p // CHUNK, CHUNK)
    wp["msp_s"] = _kron_tril(w_spatial, nb, t)
    bsp = jnp.repeat(jnp.swapaxes(b_spatial, 1, 2), DB, axis=2)
    wp["bsp_p"] = jnp.tile(bsp, (1, tm_p // CHUNK, 1))
    wp["bsp_s"] = jnp.tile(bsp[:, :t, :], (1, nb, 1))
    wp["wc"] = jnp.pad(w_conv_c, ((0, 0), (0, HALO_C - KC), (0, 0)))
    wp["bc"] = r3(b_conv_c)
    wp["lcg"] = r3(ln_conv_g)
    wp["lcb"] = r3(ln_conv_b)
    wp["wd"] = jnp.pad(w_conv_d, ((0, 0), (0, HALO_D - KD), (0, 0)))
    wp["tril_p"] = jnp.asarray(np.tril(np.ones((tm_p, tm_p), np.float32)), dtype=BF16)
    wp["wbr"] = w_branch.astype(BF16)
    wp["wout"] = w_out.astype(BF16)
    wp["ln1g"] = r3(ln1_g)
    wp["ln1b"] = r3(ln1_b)
    wr = jnp.concatenate([w_router_group, w_router_expert,
                          jnp.zeros((d, D_MODEL, LANES - N_GROUPS - N_EXPERTS), F32)], axis=-1)
    wr_hi = wr.astype(BF16)
    wp["wr_hi"] = wr_hi
    wp["wr_lo"] = (wr - wr_hi.astype(F32)).astype(BF16)
    wp["br"] = jnp.concatenate([b_router_group, b_router_expert,
                                jnp.zeros((d, LANES - N_GROUPS - N_EXPERTS), F32)], axis=-1).reshape(d, 1, LANES)
    wp["weg"] = w_exp_gate.astype(BF16)
    wp["weu"] = w_exp_up.astype(BF16)
    wp["wed"] = w_exp_down.astype(BF16)
    wp["wpg"] = w_ple_gate.astype(BF16)
    wp["wpp"] = w_ple_proj.astype(BF16)
    wp["ln2g"] = r3(ln2_g)
    wp["ln2b"] = r3(ln2_b)

    rows = n_pages * SUBLANES
    ri = np.arange(rows)
    pm = ((ri[:, None] % SUBLANES) == (ri[None, :] % SUBLANES)) & ((ri[None, :] // SUBLANES) < (ri[:, None] // SUBLANES))
    wp["pmat"] = jnp.asarray(pm.astype(np.float32), dtype=BF16)
    a8 = (np.arange(LANES)[None, :] == (np.arange(SUBLANES)[:, None] % HA)).astype(np.float32)
    wp["a8"] = jnp.asarray(a8, dtype=BF16)
    wp["ltri"] = jnp.asarray(np.tril(np.ones((LANES, LANES), np.float32)), dtype=BF16)
    rsel = ((np.arange(HA * t)[None, :] // HA) == np.arange(t)[:, None]).astype(np.float32)
    wp["rsel"] = jnp.asarray(rsel, dtype=BF16)
    rr = np.arange(PAGE * HA)
    cc = np.arange(SUBLANES * LANES)
    tm_ = ((rr[:, None] % HA) == ((cc[None, :] // LANES) % HA)) & ((rr[:, None] // HA) <= (cc[None, :] % LANES))
    wp["tpage"] = jnp.asarray(tm_.astype(np.float32), dtype=BF16)
    return wp


def kernel(x_prompt, x_sample, cache_k, cache_v, cache_logf, state_conv_c, state_conv_d, page_table,
           p_prompt, p_sample, w_in, b_forget, ln_gmlp_g, ln_gmlp_b, w_spatial, b_spatial, w_conv_c,
           b_conv_c, ln_conv_g, ln_conv_b, w_conv_d, w_branch, w_out, ln1_g, ln1_b, w_router_group,
           b_router_group, w_router_expert, b_router_expert, w_exp_gate, w_exp_up, w_exp_down,
           w_ple_gate, w_ple_proj, ln2_g, ln2_b):
    depth = w_in.shape[0]
    alpha = float((2 * depth) ** 0.25)
    b, s, _ = x_prompt.shape
    nb, t, _ = x_sample.shape
    n_pool = cache_k.shape[1]
    n_pages = page_table.shape[1]
    tm_p = min(512, s)
    tq = min(512, s)
    ms = nb * t

    wp = _prepare(w_in, b_forget, ln_gmlp_g, ln_gmlp_b, w_spatial, b_spatial, w_conv_c, b_conv_c,
                  ln_conv_g, ln_conv_b, w_conv_d, w_branch, w_out, ln1_g, ln1_b, w_router_group,
                  b_router_group, w_router_expert, b_router_expert, w_exp_gate, w_exp_up, w_exp_down,
                  w_ple_gate, w_ple_proj, ln2_g, ln2_b, tm_p, nb, t, n_pages)

    ck = cache_k.reshape(depth, n_pool, PAGE, W_BR).astype(BF16)
    cv = cache_v.reshape(depth, n_pool, PAGE, W_BR).astype(BF16)
    n_rows = depth * n_pool
    tr = 512 if n_rows % 512 == 0 else n_rows
    cc = _page_cumsum(cache_logf.reshape(n_rows, PAGE * HA), wp["tpage"], tr)
    cc = cc.reshape(depth, n_pool, SUBLANES, LANES)

    pp = p_prompt.reshape(depth, b * s, D_PLE)
    ps = p_sample.reshape(depth, ms, D_PLE)

    xp = x_prompt
    xs = x_sample.reshape(ms, D_MODEL)
    kp, vp, lfp, ccp, cdp = [], [], [], [], []
    ks_, vs_, lfs, ccs, cds, gvs = [], [], [], [], [], []
    for l in range(depth):
        (q, k32, v32, kb, vb, logf, ct, yb, yc, yd, stc, std) = _k1_prompt(l, xp, wp, tm_p)
        ya = _attn_prompt(q, kb, vb, ct, tq)
        f2 = lambda a: a.reshape(b * s, a.shape[-1])
        x1 = _merge(l, f2(xp), f2(ya), f2(yb), f2(yc), f2(yd), wp, tm_p, alpha)
        xp = _moe(l, x1, pp, wp, tm_p, alpha).reshape(b, s, D_MODEL)
        kp.append(k32.reshape(b, s, HA, DH))
        vp.append(v32.reshape(b, s, HA, DH))
        lfp.append(logf)
        ccp.append(stc[:, HALO_C - (KC - 1):, :])
        cdp.append(std[:, HALO_D - (KD - 1):, :])

        (qs, k_s, v_s, lf_s, vn_s, yb_s, yc_s, yd_s, nsc, nsd) = _k1_sample(
            l, xs, state_conv_c, state_conv_d, wp, nb, t)
        r3 = lambda a: a.reshape(nb, t, a.shape[-1])
        ya_s = _attn_sample(l, page_table, r3(qs), r3(k_s), r3(v_s), r3(lf_s), wp, ck, cv, cc)
        x1s = _merge(l, xs, ya_s.reshape(ms, W_BR), yb_s, yc_s, yd_s, wp, ms, alpha)
        xs = _moe(l, x1s, ps, wp, ms, alpha)
        ks_.append(k_s.reshape(nb, t, HA, DH))
        vs_.append(v_s.reshape(nb, t, HA, DH))
        lfs.append(lf_s[:, 0:HA].reshape(nb, t, HA))
        ccs.append(nsc)
        cds.append(nsd)
        gvs.append(vn_s.reshape(nb, t, GB, DB))

    return (xp, xs.reshape(nb, t, D_MODEL), jnp.stack(kp), jnp.stack(vp), jnp.stack(lfp),
            jnp.stack(ks_), jnp.stack(vs_), jnp.stack(lfs), jnp.stack(ccp), jnp.stack(ccs),
            jnp.stack(cdp), jnp.stack(cds), jnp.stack(gvs))
```

```python
import functools

import numpy as np
import jax
import jax.numpy as jnp
from jax import lax
from jax.experimental import pallas as pl
from jax.experimental.pallas import tpu as pltpu

F32 = jnp.float32
BF16 = jnp.bfloat16

D_MODEL = 1024
W_BR = 256
HA = 4
DH = 64
GB = 4
DB = 64
CHUNK = 128
KC = 31
KD = 3
N_BRANCH = 4
N_GROUPS = 4
EXP_PER_GROUP = 4
N_EXPERTS = 16
D_EXPERT = 256
D_PLE = 256
PAGE = 128
LN_EPS = 1e-5
ATTN_SCALE = DH ** -0.5
LOG2E = 1.4426950408889634
Q_SCALE = ATTN_SCALE * LOG2E
LANES = 128
SUBLANES = 8
HALO_C = 32
HALO_D = 8
W_MIX = 10 * W_BR + LANES
VMEM_LIMIT = 56 * 1024 * 1024


def _cparams(n_axes):
    return pltpu.CompilerParams(dimension_semantics=("arbitrary",) * n_axes,
                                vmem_limit_bytes=VMEM_LIMIT)


def _const_spec(shape, index_map):
    return pl.BlockSpec(shape, index_map, pipeline_mode=pl.Buffered(1))


def _dot(a, b):
    return jnp.dot(a, b, preferred_element_type=F32)


def _dot_nt(a, b):
    return lax.dot_general(a, b, (((1,), (1,)), ((), ())), preferred_element_type=F32)


def _split3(x):
    hi = x.astype(BF16)
    r = x - hi.astype(F32)
    mid = r.astype(BF16)
    lo = (r - mid.astype(F32)).astype(BF16)
    return hi, mid, lo


def _dot3_r(x, m):
    hi, mid, lo = _split3(x)
    return _dot(hi, m) + _dot(mid, m) + _dot(lo, m)


def _dot3_l(m, x):
    hi, mid, lo = _split3(x)
    return _dot(m, hi) + _dot(m, mid) + _dot(m, lo)


def _dot3_nt(m, x):
    hi, mid, lo = _split3(x)
    return _dot_nt(m, hi) + _dot_nt(m, mid) + _dot_nt(m, lo)


def _sigmoid(x):
    return 1.0 / (1.0 + jnp.exp(-x))


def _silu(x):
    return x * _sigmoid(x)


def _gelu(x):
    return 0.5 * x * (1.0 + jnp.tanh(0.7978845608028654 * (x + 0.044715 * (x * x * x))))


def _log_sigmoid(x):
    return jnp.minimum(x, 0.0) - jnp.log1p(jnp.exp(-jnp.abs(x)))


def _layer_norm(x, g, b):
    mu = jnp.mean(x, axis=-1, keepdims=True)
    d = x - mu
    var = jnp.mean(d * d, axis=-1, keepdims=True)
    return d * lax.rsqrt(var + LN_EPS) * g + b


def _lane_group_mask(rows, width, group, g):
    lane = lax.broadcasted_iota(jnp.int32, (rows, width), 1)
    return (lane >= g * group) & (lane < (g + 1) * group)


def _mixer_front(x, wmix_ref, bf_ref):
    z = _dot(x.astype(BF16), wmix_ref[...])
    seg = lambda i: z[:, i * W_BR:(i + 1) * W_BR]
    f = z[:, 10 * W_BR:10 * W_BR + LANES]
    lane = lax.broadcasted_iota(jnp.int32, f.shape, 1)
    logf = jnp.where(lane < HA, _log_sigmoid(f + bf_ref[...]), 0.0)
    return [seg(i) for i in range(10)], logf


def _gmlp(ub, vb, gavg_ref, lng_ref, lnb_ref, msp_ref, bsp_ref):
    rows = ub.shape[0]
    ug = _gelu(ub)
    vg = _gelu(vb)
    gavg = gavg_ref[...]
    mu = _dot3_r(vg, gavg)
    dv = vg - mu
    var = _dot3_r(dv * dv, gavg)
    vn = dv * lax.rsqrt(var + LN_EPS) * lng_ref[...] + lnb_ref[...]
    vnb = vn.astype(BF16)
    mix = bsp_ref[...]
    for g in range(GB):
        mg = _dot(msp_ref[g], vnb)
        mix = mix + jnp.where(_lane_group_mask(rows, W_BR, DB, g), mg, 0.0)
    return ug * mix, vn


def _k1_prompt_kernel(x_ref, wmix_ref, bf_ref, gavg_ref, lng_ref, lnb_ref, msp_ref, bsp_ref,
                      wc_ref, bc_ref, lcg_ref, lcb_ref, wd_ref, tril_ref,
                      q_ref, k_ref, v_ref, kb_ref, vb_ref, logf_ref, ct_ref,
                      yb_ref, yc_ref, yd_ref, sc_ref, sd_ref,
                      xpc, xpd, carry, xsh, *, tm):
    i = pl.program_id(1)

    @pl.when(i == 0)
    def _():
        xpc[0:HALO_C, :] = jnp.zeros((HALO_C, W_BR), F32)
        xpd[0:HALO_D, :] = jnp.zeros((HALO_D, W_BR), F32)
        carry[...] = jnp.zeros_like(carry)

    (q, k, v, ub, vb, ac, gc, bd, cd, hd), logf = _mixer_front(x_ref[0], wmix_ref, bf_ref)
    q_ref[0] = (q * Q_SCALE).astype(BF16)
    k_ref[0] = k
    v_ref[0] = v
    kb_ref[0] = k.astype(BF16)
    vb_ref[0] = v.astype(BF16)
    logf_ref[0] = logf[:, 0:HA]

    c = _dot3_l(tril_ref[...], logf) + carry[0:1, :]
    carry[0:1, :] = c[tm - 1:tm, :]
    ct_ref[0] = c.T[0:SUBLANES, :]

    yb, _ = _gmlp(ub, vb, gavg_ref, lng_ref, lnb_ref, msp_ref, bsp_ref)
    yb_ref[0] = yb.astype(BF16)

    cin = ac * _sigmoid(gc)
    xpc[HALO_C:HALO_C + tm, :] = cin
    span = tm + HALO_C - SUBLANES
    for r in range(1, SUBLANES):
        xsh[r - 1, 0:span, :] = xpc[r:r + span, :]
    acc = jnp.zeros((tm, W_BR), F32) + bc_ref[...]
    for j in range(KC):
        off = HALO_C - (KC - 1) + j
        r = off % SUBLANES
        a = off - r
        win = xpc[a:a + tm, :] if r == 0 else xsh[r - 1, a:a + tm, :]
        acc = acc + wc_ref[j:j + 1, :] * win
    xpc[0:HALO_C, :] = cin[tm - HALO_C:tm, :]
    sc_ref[0] = cin[tm - HALO_C:tm, :]
    yc_ref[0] = _silu(_layer_norm(acc, lcg_ref[...], lcb_ref[...])).astype(BF16)

    din = cd * hd
    xpd[HALO_D:HALO_D + tm, :] = din
    accd = jnp.zeros((tm, W_BR), F32)
    for j in range(KD):
        off = HALO_D - (KD - 1) + j
        accd = accd + wd_ref[j:j + 1, :] * xpd[off:off + tm, :]
    xpd[0:HALO_D, :] = din[tm - HALO_D:tm, :]
    sd_ref[0] = din[tm - HALO_D:tm, :]
    yd_ref[0] = (bd * accd).astype(BF16)


def _k1_prompt(l, x, wp, tm):
    b, s, _ = x.shape
    nt = s // tm
    row = lambda bb, i: (bb, i, 0)
    cst2 = lambda bb, i: (l, 0, 0)
    o256 = lambda dt: jax.ShapeDtypeStruct((b, s, W_BR), dt)
    in_specs = [
        pl.BlockSpec((1, tm, D_MODEL), row),
        _const_spec((None, D_MODEL, W_MIX), cst2),
        _const_spec((None, 1, LANES), cst2),
        _const_spec((W_BR, W_BR), lambda bb, i: (0, 0)),
        _const_spec((None, 1, W_BR), cst2),
        _const_spec((None, 1, W_BR), cst2),
        _const_spec((None, GB, tm, tm), lambda bb, i: (l, 0, 0, 0)),
        _const_spec((None, tm, W_BR), cst2),
        _const_spec((None, HALO_C, W_BR), cst2),
        _const_spec((None, 1, W_BR), cst2),
        _const_spec((None, 1, W_BR), cst2),
        _const_spec((None, 1, W_BR), cst2),
        _const_spec((None, HALO_D, W_BR), cst2),
        _const_spec((tm, tm), lambda bb, i: (0, 0)),
    ]
    out_shape = [o256(BF16), o256(F32), o256(F32), o256(BF16), o256(BF16),
                 jax.ShapeDtypeStruct((b, s, HA), F32),
                 jax.ShapeDtypeStruct((b, SUBLANES, s), F32),
                 o256(BF16), o256(BF16), o256(BF16),
                 jax.ShapeDtypeStruct((b, HALO_C, W_BR), F32),
                 jax.ShapeDtypeStruct((b, HALO_D, W_BR), F32)]
    blk = pl.BlockSpec((1, tm, W_BR), row)
    out_specs = [blk, blk, blk, blk, blk,
                 pl.BlockSpec((1, tm, HA), row),
                 pl.BlockSpec((1, SUBLANES, tm), lambda bb, i: (bb, 0, i)),
                 blk, blk, blk,
                 pl.BlockSpec((1, HALO_C, W_BR), lambda bb, i: (bb, 0, 0)),
                 pl.BlockSpec((1, HALO_D, W_BR), lambda bb, i: (bb, 0, 0))]
    return pl.pallas_call(
        functools.partial(_k1_prompt_kernel, tm=tm),
        grid=(b, nt), in_specs=in_specs, out_specs=out_specs, out_shape=out_shape,
        scratch_shapes=[pltpu.VMEM((tm + HALO_C, W_BR), F32), pltpu.VMEM((tm + HALO_D, W_BR), F32),
                        pltpu.VMEM((SUBLANES, LANES), F32),
                        pltpu.VMEM((SUBLANES - 1, tm + HALO_C - SUBLANES, W_BR), F32)],
        compiler_params=_cparams(2), name="k1_prompt",
    )(x, wp["wmix"], wp["bf"], wp["gavg"], wp["lng"], wp["lnb"], wp["msp_p"], wp["bsp_p"],
      wp["wc"], wp["bc"], wp["lcg"], wp["lcb"], wp["wd"], wp["tril_p"])


def _k1_sample_kernel(x_ref, wmix_ref, bf_ref, gavg_ref, lng_ref, lnb_ref, msp_ref, bsp_ref,
                      wc_ref, bc_ref, lcg_ref, lcb_ref, wd_ref, stc_ref, std_ref,
                      q_ref, k_ref, v_ref, logf_ref, vn_ref, yb_ref, yc_ref, yd_ref, nsc_ref, nsd_ref,
                      cin_s, ypre_s, din_s, ydpre_s, xs, xsd, *, nb, t):
    (q, k, v, ub, vb, ac, gc, bd, cd, hd), logf = _mixer_front(x_ref[...], wmix_ref, bf_ref)
    q_ref[...] = q * Q_SCALE
    k_ref[...] = k
    v_ref[...] = v
    logf_ref[...] = logf
    yb, vn = _gmlp(ub, vb, gavg_ref, lng_ref, lnb_ref, msp_ref, bsp_ref)
    yb_ref[...] = yb.astype(BF16)
    vn_ref[...] = vn

    cin_s[...] = ac * _sigmoid(gc)
    din_s[...] = cd * hd

    def body(bb, _):
        r0 = pl.multiple_of(bb * t, t)
        xs[0:KC - 1, :] = stc_ref[bb]
        xs[KC - 1:KC - 1 + t, :] = cin_s[pl.ds(r0, t), :]
        acc = jnp.zeros((t, W_BR), F32) + bc_ref[...]
        for j in range(KC):
            acc = acc + wc_ref[j:j + 1, :] * xs[j:j + t, :]
        ypre_s[pl.ds(r0, t), :] = acc
        nsc_ref[bb] = xs[t:t + KC - 1, :]

        xsd[0:KD - 1, :] = std_ref[bb]
        xsd[KD - 1:KD - 1 + t, :] = din_s[pl.ds(r0, t), :]
        accd = jnp.zeros((t, W_BR), F32)
        for j in range(KD):
            accd = accd + wd_ref[j:j + 1, :] * xsd[j:j + t, :]
        ydpre_s[pl.ds(r0, t), :] = accd
        nsd_ref[bb] = xsd[t:t + KD - 1, :]
        return 0

    lax.fori_loop(0, nb, body, 0)
    yc_ref[...] = _silu(_layer_norm(ypre_s[...], lcg_ref[...], lcb_ref[...])).astype(BF16)
    yd_ref[...] = (bd * ydpre_s[...]).astype(BF16)


def _k1_sample(l, x, stc, std, wp, nb, t):
    m = nb * t
    cst2 = lambda i: (l, 0, 0)
    cst4 = lambda i: (l, 0, 0, 0)
    full = lambda shp: pl.BlockSpec(shp, lambda i: (0,) * len(shp))
    in_specs = [
        full((m, D_MODEL)),
        pl.BlockSpec((None, D_MODEL, W_MIX), cst2),
        pl.BlockSpec((None, 1, LANES), cst2),
        full((W_BR, W_BR)),
        pl.BlockSpec((None, 1, W_BR), cst2),
        pl.BlockSpec((None, 1, W_BR), cst2),
        pl.BlockSpec((None, GB, m, m), cst4),
        pl.BlockSpec((None, m, W_BR), cst2),
        pl.BlockSpec((None, HALO_C, W_BR), cst2),
        pl.BlockSpec((None, 1, W_BR), cst2),
        pl.BlockSpec((None, 1, W_BR), cst2),
        pl.BlockSpec((None, 1, W_BR), cst2),
        pl.BlockSpec((None, HALO_D, W_BR), cst2),
        pl.BlockSpec((None, nb, KC - 1, W_BR), cst4),
        pl.BlockSpec((None, nb, KD - 1, W_BR), cst4),
    ]
    o = lambda dt: jax.ShapeDtypeStruct((m, W_BR), dt)
    out_shape = [o(F32), o(F32), o(F32), jax.ShapeDtypeStruct((m, LANES), F32), o(F32),
                 o(BF16), o(BF16), o(BF16),
                 jax.ShapeDtypeStruct((nb, KC - 1, W_BR), F32),
                 jax.ShapeDtypeStruct((nb, KD - 1, W_BR), F32)]
    out_specs = [full(s.shape) for s in out_shape]
    return pl.pallas_call(
        functools.partial(_k1_sample_kernel, nb=nb, t=t),
        grid=(1,), in_specs=in_specs, out_specs=out_specs, out_shape=out_shape,
        scratch_shapes=[pltpu.VMEM((m, W_BR), F32), pltpu.VMEM((m, W_BR), F32),
                        pltpu.VMEM((m, W_BR), F32), pltpu.VMEM((m, W_BR), F32),
                        pltpu.VMEM((KC - 1 + t + 2, W_BR), F32), pltpu.VMEM((2 * SUBLANES, W_BR), F32)],
        compiler_params=pltpu.CompilerParams(vmem_limit_bytes=VMEM_LIMIT), name="k1_sample",
    )(x, wp["wmix"], wp["bf"], wp["gavg"], wp["lng"], wp["lnb"], wp["msp_s"], wp["bsp_s"],
      wp["wc"], wp["bc"], wp["lcg"], wp["lcb"], wp["wd"], stc, std)


def _attn_prompt_kernel(qi_tab, ki_tab, q_ref, k_ref, v_ref, ck_ref, cq_ref, o_ref,
                        qm_s, m_s, l_s, acc_s, *, tq, tk):
    p = pl.program_id(1)
    qi = qi_tab[p]
    ki = ki_tab[p]
    last_k = ((qi + 1) * tq - 1) // tk

    @pl.when(ki == 0)
    def _():
        q = q_ref[0]
        for h in range(HA):
            qm_s[h] = jnp.where(_lane_group_mask(tq, W_BR, DH, h), q, jnp.zeros_like(q))
        m_s[...] = jnp.full(m_s.shape, -jnp.inf, F32)
        l_s[...] = jnp.zeros(l_s.shape, F32)
        acc_s[...] = jnp.zeros(acc_s.shape, F32)

    def step(masked):
        k = k_ref[0]
        v = v_ref[0]
        ck = ck_ref[0]
        cq = cq_ref[0]
        if masked:
            row = qi * tq + lax.broadcasted_iota(jnp.int32, (tq, tk), 0)
            col = ki * tk + lax.broadcasted_iota(jnp.int32, (tq, tk), 1)
            keep = col <= row
        for h in range(HA):
            bias = (cq[h:h + 1, 0:1] - ck[h:h + 1, :]) * LOG2E
            s = _dot_nt(qm_s[h], k) + bias
            if masked:
                s = jnp.where(keep, s, -jnp.inf)
            m_prev = m_s[h]
            m_new = jnp.maximum(m_prev, jnp.max(s, axis=1, keepdims=True))
            alpha = jnp.exp2(m_prev - m_new)
            pr = jnp.exp2(s - m_new)
            l_s[h] = alpha * l_s[h] + jnp.sum(pr, axis=1, keepdims=True)
            acc_s[h] = acc_s[h] * alpha + _dot(pr.astype(BF16), v)
            m_s[h] = m_new

    needs_mask = (ki + 1) * tk - 1 > qi * tq

    @pl.when(jnp.logical_not(needs_mask))
    def _():
        step(False)

    @pl.when(needs_mask)
    def _():
        step(True)

    @pl.when(ki == last_k)
    def _():
        out = jnp.zeros((tq, W_BR), F32)
        for h in range(HA):
            out = out + jnp.where(_lane_group_mask(tq, W_BR, DH, h), acc_s[h] / l_s[h], 0.0)
        o_ref[0] = out.astype(BF16)


def _attn_prompt(q, k, v, ct, tq, tk):
    b, s, _ = q.shape
    nq = s // tq
    n_k = [((i + 1) * tq - 1) // tk + 1 for i in range(nq)]
    qi_np = np.concatenate([np.full(n_k[i], i, np.int32) for i in range(nq)])
    ki_np = np.concatenate([np.arange(n_k[i], dtype=np.int32) for i in range(nq)])
    n_pairs = int(qi_np.shape[0])
    grid_spec = pltpu.PrefetchScalarGridSpec(
        num_scalar_prefetch=2, grid=(b, n_pairs),
        in_specs=[
            pl.BlockSpec((1, tq, W_BR), lambda bb, p, qt, kt: (bb, qt[p], 0)),
            pl.BlockSpec((1, tk, W_BR), lambda bb, p, qt, kt: (bb, kt[p], 0)),
            pl.BlockSpec((1, tk, W_BR), lambda bb, p, qt, kt: (bb, kt[p], 0)),
            pl.BlockSpec((1, SUBLANES, tk), lambda bb, p, qt, kt: (bb, 0, kt[p])),
            pl.BlockSpec((1, SUBLANES, tq), lambda bb, p, qt, kt: (bb, 0, qt[p])),
        ],
        out_specs=pl.BlockSpec((1, tq, W_BR), lambda bb, p, qt, kt: (bb, qt[p], 0)),
        scratch_shapes=[pltpu.VMEM((HA, tq, W_BR), BF16), pltpu.VMEM((HA, tq, 1), F32),
                        pltpu.VMEM((HA, tq, 1), F32), pltpu.VMEM((HA, tq, W_BR), F32)],
    )
    return pl.pallas_call(
        functools.partial(_attn_prompt_kernel, tq=tq, tk=tk),
        grid_spec=grid_spec, out_shape=jax.ShapeDtypeStruct((b, s, W_BR), BF16),
        compiler_params=_cparams(2), name="attn_prompt",
    )(jnp.asarray(qi_np), jnp.asarray(ki_np), q, k, v, ct, ct)


def _page_cumsum_kernel(lf_ref, u_ref, d_ref, o_ref):
    wc = _dot3_r(lf_ref[...], u_ref[...])
    o_ref[...] = _dot3_l(d_ref[...], wc)


def _page_cumsum(lf_rows, upper, dsel, pps):
    n_pg = lf_rows.shape[0] // HA
    return pl.pallas_call(
        _page_cumsum_kernel, grid=(n_pg // pps,),
        in_specs=[pl.BlockSpec((pps * HA, PAGE), lambda i: (i, 0)),
                  _const_spec((PAGE, PAGE), lambda i: (0, 0)),
                  _const_spec((pps * SUBLANES, pps * HA), lambda i: (0, 0))],
        out_specs=pl.BlockSpec((pps * SUBLANES, PAGE), lambda i: (i, 0)),
        out_shape=jax.ShapeDtypeStruct((n_pg * SUBLANES, PAGE), F32),
        compiler_params=_cparams(1), name="page_cumsum",
    )(lf_rows, upper, dsel)


def _attn_sample_kernel(pt_ref, q_ref, kn_ref, vn_ref, lfn_ref, pmat_ref, a8_ref, ltri_ref, rsel_ref,
                        ck_hbm, cv_hbm, cc_hbm, o_ref, kbuf, vbuf, cbuf, sem, m_s, l_s, acc_s, carry_s,
                        *, l, nb, t, n_pages, pc):
    b = pl.program_id(0)
    c = pl.program_id(1)
    nc = n_pages // pc
    step_id = b * nc + c
    slot = step_id % 2
    rows = HA * t
    ng = rows // SUBLANES

    def copies(bb, cc, sl, p):
        pg = pt_ref[bb, cc * pc + p]
        return (pltpu.make_async_copy(ck_hbm.at[l, pg], kbuf.at[sl, p], sem.at[sl, 0]),
                pltpu.make_async_copy(cv_hbm.at[l, pg], vbuf.at[sl, p], sem.at[sl, 1]),
                pltpu.make_async_copy(cc_hbm.at[l, pg], cbuf.at[sl, p], sem.at[sl, 2]))

    def issue(bb, cc, sl):
        def body(p, _):
            for cp in copies(bb, cc, sl, p):
                cp.start()
            return 0
        lax.fori_loop(0, pc, body, 0)

    @pl.when(step_id == 0)
    def _():
        issue(0, 0, 0)

    @pl.when(step_id + 1 < nb * nc)
    def _():
        nxt = step_id + 1
        issue(nxt // nc, nxt % nc, 1 - slot)

    def wait_body(p, _):
        for cp in copies(b, c, slot, p):
            cp.wait()
        return 0
    lax.fori_loop(0, pc, wait_body, 0)

    @pl.when(c == 0)
    def _():
        m_s[...] = jnp.full(m_s.shape, -jnp.inf, F32)
        l_s[...] = jnp.zeros(l_s.shape, F32)
        acc_s[...] = jnp.zeros(acc_s.shape, F32)
        carry_s[...] = jnp.zeros(carry_s.shape, F32)

    q = q_ref[0]
    sub = lax.broadcasted_iota(jnp.int32, (SUBLANES, W_BR), 0)
    lane = lax.broadcasted_iota(jnp.int32, (SUBLANES, W_BR), 1)
    head_ok = (lane // DH) == (sub % HA)
    qm = []
    for g in range(t // 2):
        qa = jnp.broadcast_to(q[2 * g:2 * g + 1, :], (SUBLANES, W_BR))
        qb = jnp.broadcast_to(q[2 * g + 1:2 * g + 2, :], (SUBLANES, W_BR))
        qm.append(jnp.where(head_ok, jnp.where(sub < HA, qa, qb), 0.0))
    qm = jnp.concatenate(qm, axis=0).astype(BF16)

    def online_update(s3, pv_fn):
        m_prev = m_s[...]
        m_new = jnp.maximum(m_prev, jnp.max(s3, axis=2, keepdims=True))
        alpha = jnp.exp2(m_prev - m_new)
        pr = jnp.exp2(s3 - m_new)
        l_s[...] = alpha * l_s[...] + jnp.sum(pr, axis=2, keepdims=True)
        acc_s[...] = acc_s[...] * alpha.reshape(rows, 1) + pv_fn(pr.reshape(rows, s3.shape[2]).astype(BF16))
        m_s[...] = m_new

    c2 = cbuf[slot].reshape(pc * SUBLANES, LANES)
    off = _dot3_l(pmat_ref[...], c2)
    carry = jnp.concatenate([carry_s[:, 0:1]] * pc, axis=0)
    cf = c2 + off[:, LANES - 1:LANES] + carry
    tot = cf[(pc - 1) * SUBLANES:pc * SUBLANES, LANES - 1:LANES]
    carry_s[...] = jnp.broadcast_to(tot, carry_s.shape)
    c_past = jnp.concatenate([cf[j * SUBLANES:(j + 1) * SUBLANES, :] for j in range(pc)], axis=1)

    s_past = jnp.concatenate([_dot(qm, kbuf[slot, j].astype(BF16)) for j in range(pc)], axis=1)
    s_past = s_past.reshape(ng, SUBLANES, pc * PAGE) - (c_past * LOG2E)[None]

    def pv_past(p2):
        acc = jnp.zeros((rows, W_BR), F32)
        for j in range(pc):
            acc = acc + _dot_nt(p2[:, j * PAGE:(j + 1) * PAGE], vbuf[slot, j].astype(BF16))
        return acc

    online_update(s_past, pv_past)

    @pl.when(c == nc - 1)
    def _():
        lfp = jnp.concatenate([lfn_ref[0], jnp.zeros((LANES - t, LANES), F32)], axis=0)
        cum_new = _dot3_l(ltri_ref[...], lfp)
        c_new = _dot3_nt(a8_ref[...], cum_new) + tot
        zpad = jnp.zeros((LANES - t, W_BR), F32)
        kn = jnp.concatenate([kn_ref[0], zpad], axis=0).astype(BF16)
        vn = jnp.concatenate([vn_ref[0], zpad], axis=0).astype(BF16)
        s_new = _dot_nt(qm, kn).reshape(ng, SUBLANES, LANES) - (c_new * LOG2E)[None]
        gi = lax.broadcasted_iota(jnp.int32, s_new.shape, 0)
        si = lax.broadcasted_iota(jnp.int32, s_new.shape, 1)
        ji = lax.broadcasted_iota(jnp.int32, s_new.shape, 2)
        s_new = jnp.where(ji <= 2 * gi + si // HA, s_new, -jnp.inf)
        online_update(s_new, lambda p2: _dot(p2, vn))

        o = acc_s[...] / l_s[...].reshape(rows, 1)
        sub_r = lax.broadcasted_iota(jnp.int32, (rows, W_BR), 0)
        lane_r = lax.broadcasted_iota(jnp.int32, (rows, W_BR), 1)
        om = jnp.where((lane_r // DH) == (sub_r % HA), o, 0.0)
        o_ref[0] = _dot(rsel_ref[...], om.astype(BF16))


def _attn_sample(l, page_table, q, kn, vn, lfn, wp, ck, cv, cc, pc):
    nb, t, _ = q.shape
    n_pages = page_table.shape[1]
    rows = HA * t
    cst = lambda shp: pl.BlockSpec(shp, lambda b, c, pt: (0,) * len(shp))
    per_b = lambda w: pl.BlockSpec((1, t, w), lambda b, c, pt: (b, 0, 0))
    grid_spec = pltpu.PrefetchScalarGridSpec(
        num_scalar_prefetch=1, grid=(nb, n_pages // pc),
        in_specs=[per_b(W_BR), per_b(W_BR), per_b(W_BR), per_b(LANES),
                  cst((pc * SUBLANES, pc * SUBLANES)), cst((SUBLANES, LANES)),
                  cst((LANES, LANES)), cst((t, rows)),
                  pl.BlockSpec(memory_space=pl.ANY), pl.BlockSpec(memory_space=pl.ANY),
                  pl.BlockSpec(memory_space=pl.ANY)],
        out_specs=per_b(W_BR),
        scratch_shapes=[pltpu.VMEM((2, pc, W_BR, PAGE), F32),
                        pltpu.VMEM((2, pc, W_BR, PAGE), F32),
                        pltpu.VMEM((2, pc, SUBLANES, LANES), F32),
                        pltpu.SemaphoreType.DMA((2, 3)),
                        pltpu.VMEM((rows // SUBLANES, SUBLANES, 1), F32),
                        pltpu.VMEM((rows // SUBLANES, SUBLANES, 1), F32),
                        pltpu.VMEM((rows, W_BR), F32),
                        pltpu.VMEM((SUBLANES, LANES), F32)],
    )
    return pl.pallas_call(
        functools.partial(_attn_sample_kernel, l=l, nb=nb, t=t, n_pages=n_pages, pc=pc),
        grid_spec=grid_spec, out_shape=jax.ShapeDtypeStruct((nb, t, W_BR), F32),
        compiler_params=_cparams(2), name="attn_sample",
    )(page_table, q, kn, vn, lfn, wp["pmat"], wp["a8"], wp["ltri"], wp["rsel"], ck, cv, cc)


def _merge_kernel(x_ref, ya_ref, yb_ref, yc_ref, yd_ref, wg_ref, wbr_ref, wo_ref, g_ref, b_ref, o_ref,
                  *, alpha):
    x = x_ref[...]
    xb = x.astype(BF16)
    ys = (ya_ref, yb_ref, yc_ref, yd_ref)
    acc = jnp.zeros(x.shape, F32)
    for n in range(N_BRANCH):
        gate = _sigmoid(_dot(xb, wg_ref[:, n * D_MODEL:(n + 1) * D_MODEL]))
        acc = acc + gate * _dot(ys[n][...].astype(BF16), wbr_ref[n])
    mix = _dot(acc.astype(BF16), wo_ref[...])
    o_ref[...] = _layer_norm(alpha * x + mix, g_ref[...], b_ref[...])


def _merge(l, x, ya, yb, yc, yd, wp, tm, alpha):
    m = x.shape[0]
    row = lambda w: pl.BlockSpec((tm, w), lambda i: (i, 0))
    c3 = lambda i: (l, 0, 0)
    return pl.pallas_call(
        functools.partial(_merge_kernel, alpha=alpha),
        grid=(m // tm,),
        in_specs=[row(D_MODEL), row(W_BR), row(W_BR), row(W_BR), row(W_BR),
                  _const_spec((None, D_MODEL, N_BRANCH * D_MODEL), c3),
                  _const_spec((None, N_BRANCH, W_BR, D_MODEL), lambda i: (l, 0, 0, 0)),
                  _const_spec((None, D_MODEL, D_MODEL), c3),
                  _const_spec((None, 1, D_MODEL), c3), _const_spec((None, 1, D_MODEL), c3)],
        out_specs=row(D_MODEL), out_shape=jax.ShapeDtypeStruct((m, D_MODEL), F32),
        compiler_params=_cparams(1), name="merge",
    )(x, ya, yb, yc, yd, wp["wgate"], wp["wbr"], wp["wout"], wp["ln1g"], wp["ln1b"])


def _moe_kernel(x_ref, ple_ref, wrh_ref, wrl_ref, br_ref, wg_ref, wu_ref, wd_ref, wpg_ref, wpp_ref,
                g_ref, b_ref, o_ref, *, alpha):
    x = x_ref[...]
    tm = x.shape[0]
    xb = x.astype(BF16)
    xm = (x - xb.astype(F32)).astype(BF16)

    logits = _dot(xb, wrh_ref[...]) + _dot(xb, wrl_ref[...]) + _dot(xm, wrh_ref[...]) + br_ref[...]
    lane = lax.broadcasted_iota(jnp.int32, (tm, LANES), 1)
    big = jnp.int32(LANES)
    lg = jnp.where(lane < N_GROUPS, logits, -jnp.inf)
    mg = jnp.max(lg, axis=1, keepdims=True)
    p_top = 1.0 / jnp.sum(jnp.exp(lg - mg), axis=1, keepdims=True)
    g_idx = jnp.min(jnp.where(lg == mg, lane, big), axis=1, keepdims=True)
    lo = N_GROUPS + g_idx * EXP_PER_GROUP
    le = jnp.where((lane >= lo) & (lane < lo + EXP_PER_GROUP), logits, -jnp.inf)
    v1 = jnp.max(le, axis=1, keepdims=True)
    i1 = jnp.min(jnp.where(le == v1, lane, big), axis=1, keepdims=True)
    le2 = jnp.where(lane == i1, -jnp.inf, le)
    v2 = jnp.max(le2, axis=1, keepdims=True)
    i2 = jnp.min(jnp.where(le2 == v2, lane, big), axis=1, keepdims=True)
    e2 = jnp.exp(v2 - v1)
    w1 = p_top / (1.0 + e2)
    w2 = p_top * e2 / (1.0 + e2)
    comb = jnp.where(lane == i1, w1, 0.0) + jnp.where(lane == i2, w2, 0.0)

    acc = jnp.zeros(x.shape, F32)
    for e in range(N_EXPERTS):
        hg = _dot(xb, wg_ref[e])
        hu = _dot(xb, wu_ref[e])
        hh = _silu(hg) * hu * comb[:, N_GROUPS + e:N_GROUPS + e + 1]
        acc = acc + _dot(hh.astype(BF16), wd_ref[e])

    ple_term = _sigmoid(_dot(xb, wpg_ref[...])) * _dot(ple_ref[...].astype(BF16), wpp_ref[...])
    o_ref[...] = _layer_norm(alpha * x + acc + ple_term, g_ref[...], b_ref[...])


def _moe(l, x, ple, wp, tm, alpha):
    m = x.shape[0]
    c3 = lambda i: (l, 0, 0)
    c4 = lambda i: (l, 0, 0, 0)
    return pl.pallas_call(
        functools.partial(_moe_kernel, alpha=alpha),
        grid=(m // tm,),
        in_specs=[pl.BlockSpec((tm, D_MODEL), lambda i: (i, 0)),
                  pl.BlockSpec((None, tm, D_PLE), lambda i: (l, i, 0)),
                  _const_spec((None, D_MODEL, LANES), c3), _const_spec((None, D_MODEL, LANES), c3),
                  _const_spec((None, 1, LANES), c3),
                  _const_spec((None, N_EXPERTS, D_MODEL, D_EXPERT), c4),
                  _const_spec((None, N_EXPERTS, D_MODEL, D_EXPERT), c4),
                  _const_spec((None, N_EXPERTS, D_EXPERT, D_MODEL), c4),
                  _const_spec((None, D_MODEL, D_MODEL), c3), _const_spec((None, D_PLE, D_MODEL), c3),
                  _const_spec((None, 1, D_MODEL), c3), _const_spec((None, 1, D_MODEL), c3)],
        out_specs=pl.BlockSpec((tm, D_MODEL), lambda i: (i, 0)),
        out_shape=jax.ShapeDtypeStruct((m, D_MODEL), F32),
        compiler_params=_cparams(1), name="moe",
    )(x, ple, wp["wr_hi"], wp["wr_lo"], wp["br"], wp["weg"], wp["weu"], wp["wed"],
      wp["wpg"], wp["wpp"], wp["ln2g"], wp["ln2b"])


def _kron_tril(w_spatial, n_blocks, n):
    w = jnp.tril(w_spatial)[:, :, :n, :n]
    eye = jnp.eye(n_blocks, dtype=w.dtype)
    out = jnp.einsum("ab,lgij->lgaibj", eye, w)
    d = w_spatial.shape[0]
    return out.reshape(d, GB, n_blocks * n, n_blocks * n).astype(BF16)


def _prepare(w_in, b_forget, ln_gmlp_g, ln_gmlp_b, w_spatial, b_spatial, w_conv_c, b_conv_c, ln_conv_g,
             ln_conv_b, w_conv_d, w_branch, w_out, ln1_g, ln1_b, w_router_group, b_router_group,
             w_router_expert, b_router_expert, w_exp_gate, w_exp_up, w_exp_down, w_ple_gate, w_ple_proj,
             ln2_g, ln2_b, tm_p, nb, t, pc, pps):
    d = w_in.shape[0]
    o_f = 3 * W_BR
    o_rest = o_f + HA
    o_gate = o_rest + 7 * W_BR
    r3 = lambda a: a.reshape(d, 1, -1)
    wp = {}
    wp["wmix"] = jnp.concatenate(
        [w_in[:, :, 0:o_f], w_in[:, :, o_rest:o_gate],
         jnp.pad(w_in[:, :, o_f:o_rest], ((0, 0), (0, 0), (0, LANES - HA)))], axis=-1).astype(BF16)
    wp["wgate"] = w_in[:, :, o_gate:].astype(BF16)
    wp["bf"] = jnp.pad(b_forget, ((0, 0), (0, LANES - HA))).reshape(d, 1, LANES)
    gi = np.arange(W_BR) // DB
    wp["gavg"] = jnp.asarray((gi[:, None] == gi[None, :]).astype(np.float32) / DB, dtype=BF16)
    wp["lng"] = r3(ln_gmlp_g)
    wp["lnb"] = r3(ln_gmlp_b)
    wp["msp_p"] = _kron_tril(w_spatial, tm_p // CHUNK, CHUNK)
    wp["msp_s"] = _kron_tril(w_spatial, nb, t)
    bsp = jnp.repeat(jnp.swapaxes(b_spatial, 1, 2), DB, axis=2)
    wp["bsp_p"] = jnp.tile(bsp, (1, tm_p // CHUNK, 1))
    wp["bsp_s"] = jnp.tile(bsp[:, :t, :], (1, nb, 1))
    wp["wc"] = jnp.pad(w_conv_c, ((0, 0), (0, HALO_C - KC), (0, 0)))
    wp["bc"] = r3(b_conv_c)
    wp["lcg"] = r3(ln_conv_g)
    wp["lcb"] = r3(ln_conv_b)
    wp["wd"] = jnp.pad(w_conv_d, ((0, 0), (0, HALO_D - KD), (0, 0)))
    wp["tril_p"] = jnp.asarray(np.tril(np.ones((tm_p, tm_p), np.float32)), dtype=BF16)
    wp["wbr"] = w_branch.astype(BF16)
    wp["wout"] = w_out.astype(BF16)
    wp["ln1g"] = r3(ln1_g)
    wp["ln1b"] = r3(ln1_b)
    wr = jnp.concatenate([w_router_group, w_router_expert,
                          jnp.zeros((d, D_MODEL, LANES - N_GROUPS - N_EXPERTS), F32)], axis=-1)
    wr_hi = wr.astype(BF16)
    wp["wr_hi"] = wr_hi
    wp["wr_lo"] = (wr - wr_hi.astype(F32)).astype(BF16)
    wp["br"] = jnp.concatenate([b_router_group, b_router_expert,
                                jnp.zeros((d, LANES - N_GROUPS - N_EXPERTS), F32)], axis=-1).reshape(d, 1, LANES)
    wp["weg"] = w_exp_gate.astype(BF16)
    wp["weu"] = w_exp_up.astype(BF16)
    wp["wed"] = w_exp_down.astype(BF16)
    wp["wpg"] = w_ple_gate.astype(BF16)
    wp["wpp"] = w_ple_proj.astype(BF16)
    wp["ln2g"] = r3(ln2_g)
    wp["ln2b"] = r3(ln2_b)

    rows = pc * SUBLANES
    ri = np.arange(rows)
    pm = ((ri[:, None] % SUBLANES) == (ri[None, :] % SUBLANES)) & ((ri[None, :] // SUBLANES) < (ri[:, None] // SUBLANES))
    wp["pmat"] = jnp.asarray(pm.astype(np.float32), dtype=BF16)
    a8 = (np.arange(LANES)[None, :] == (np.arange(SUBLANES)[:, None] % HA)).astype(np.float32)
    wp["a8"] = jnp.asarray(a8, dtype=BF16)
    wp["ltri"] = jnp.asarray(np.tril(np.ones((LANES, LANES), np.float32)), dtype=BF16)
    rsel = ((np.arange(HA * t)[None, :] // HA) == np.arange(t)[:, None]).astype(np.float32)
    wp["rsel"] = jnp.asarray(rsel, dtype=BF16)
    wp["upper"] = jnp.asarray(np.triu(np.ones((PAGE, PAGE), np.float32)), dtype=BF16)
    ro = np.arange(pps * SUBLANES)
    ci = np.arange(pps * HA)
    dsel = ((ro[:, None] // SUBLANES) == (ci[None, :] // HA)) & ((ro[:, None] % HA) == (ci[None, :] % HA))
    wp["dsel"] = jnp.asarray(dsel.astype(np.float32), dtype=BF16)
    return wp


def kernel(x_prompt, x_sample, cache_k, cache_v, cache_logf, state_conv_c, state_conv_d, page_table,
           p_prompt, p_sample, w_in, b_forget, ln_gmlp_g, ln_gmlp_b, w_spatial, b_spatial, w_conv_c,
           b_conv_c, ln_conv_g, ln_conv_b, w_conv_d, w_branch, w_out, ln1_g, ln1_b, w_router_group,
           b_router_group, w_router_expert, b_router_expert, w_exp_gate, w_exp_up, w_exp_down,
           w_ple_gate, w_ple_proj, ln2_g, ln2_b):
    depth = w_in.shape[0]
    alpha = float((2 * depth) ** 0.25)
    b, s, _ = x_prompt.shape
    nb, t, _ = x_sample.shape
    n_pool = cache_k.shape[1]
    n_pages = page_table.shape[1]
    tm_p = min(512, s)
    tq, tk = min(512, s), min(1024, s)
    ms = nb * t
    pc = min(32, n_pages)
    n_pg = depth * n_pool
    pps = next(c for c in (128, 64, 32, 16, 8, 4, 2, 1) if n_pg % c == 0)

    wp = _prepare(w_in, b_forget, ln_gmlp_g, ln_gmlp_b, w_spatial, b_spatial, w_conv_c, b_conv_c,
                  ln_conv_g, ln_conv_b, w_conv_d, w_branch, w_out, ln1_g, ln1_b, w_router_group,
                  b_router_group, w_router_expert, b_router_expert, w_exp_gate, w_exp_up, w_exp_down,
                  w_ple_gate, w_ple_proj, ln2_g, ln2_b, tm_p, nb, t, pc, pps)

    ck = jnp.transpose(cache_k, (0, 1, 3, 4, 2)).reshape(depth, n_pool, W_BR, PAGE)
    cv = jnp.transpose(cache_v, (0, 1, 3, 4, 2)).reshape(depth, n_pool, W_BR, PAGE)
    lf_rows = jnp.transpose(cache_logf, (0, 1, 3, 2)).reshape(n_pg * HA, PAGE)
    cc = _page_cumsum(lf_rows, wp["upper"], wp["dsel"], pps)
    cc = cc.reshape(depth, n_pool, SUBLANES, LANES)

    pp = p_prompt.reshape(depth, b * s, D_PLE)
    ps = p_sample.reshape(depth, ms, D_PLE)

    xp = x_prompt
    xs = x_sample.reshape(ms, D_MODEL)
    kp, vp, lfp, ccp, cdp = [], [], [], [], []
    ks_, vs_, lfs, ccs, cds, gvs = [], [], [], [], [], []
    for l in range(depth):
        (q, k32, v32, kb, vb, logf, ct, yb, yc, yd, stc, std) = _k1_prompt(l, xp, wp, tm_p)
        ya = _attn_prompt(q, kb, vb, ct, tq, tk)
        f2 = lambda a: a.reshape(b * s, a.shape[-1])
        x1 = _merge(l, f2(xp), f2(ya), f2(yb), f2(yc), f2(yd), wp, tm_p, alpha)
        xp = _moe(l, x1, pp, wp, tm_p, alpha).reshape(b, s, D_MODEL)
        kp.append(k32.reshape(b, s, HA, DH))
        vp.append(v32.reshape(b, s, HA, DH))
        lfp.append(logf)
        ccp.append(stc[:, HALO_C - (KC - 1):, :])
        cdp.append(std[:, HALO_D - (KD - 1):, :])

        (qs, k_s, v_s, lf_s, vn_s, yb_s, yc_s, yd_s, nsc, nsd) = _k1_sample(
            l, xs, state_conv_c, state_conv_d, wp, nb, t)
        r3 = lambda a: a.reshape(nb, t, a.shape[-1])
        ya_s = _attn_sample(l, page_table, r3(qs), r3(k_s), r3(v_s), r3(lf_s), wp, ck, cv, cc, pc)
        x1s = _merge(l, xs, ya_s.reshape(ms, W_BR), yb_s, yc_s, yd_s, wp, ms, alpha)
        xs = _moe(l, x1s, ps, wp, ms, alpha)
        ks_.append(k_s.reshape(nb, t, HA, DH))
        vs_.append(v_s.reshape(nb, t, HA, DH))
        lfs.append(lf_s[:, 0:HA].reshape(nb, t, HA))
        ccs.append(nsc)
        cds.append(nsd)
        gvs.append(vn_s.reshape(nb, t, GB, DB))

    return (xp, xs.reshape(nb, t, D_MODEL), jnp.stack(kp), jnp.stack(vp), jnp.stack(lfp),
            jnp.stack(ks_), jnp.stack(vs_), jnp.stack(lfs), jnp.stack(ccp), jnp.stack(ccs),
            jnp.stack(cdp), jnp.stack(cds), jnp.stack(gvs))
```

```python
import functools

import numpy as np
import jax
import jax.numpy as jnp
from jax import lax
from jax.experimental import pallas as pl
from jax.experimental.pallas import tpu as pltpu

F32 = jnp.float32
BF16 = jnp.bfloat16

D_MODEL = 1024
W_BR = 256
HA = 4
DH = 64
GB = 4
DB = 64
CHUNK = 128
KC = 31
KD = 3
N_BRANCH = 4
N_GROUPS = 4
EXP_PER_GROUP = 4
N_EXPERTS = 16
D_EXPERT = 256
D_PLE = 256
PAGE = 128
LN_EPS = 1e-5
ATTN_SCALE = DH ** -0.5
LOG2E = 1.4426950408889634
Q_SCALE = ATTN_SCALE * LOG2E
LANES = 128
SUBLANES = 8
HALO_C = 32
HALO_D = 8
W_MIX = 10 * W_BR + LANES
VMEM_LIMIT = 56 * 1024 * 1024


def _cparams(n_axes):
    return pltpu.CompilerParams(dimension_semantics=("arbitrary",) * n_axes,
                                vmem_limit_bytes=VMEM_LIMIT)


def _const_spec(shape, index_map):
    return pl.BlockSpec(shape, index_map, pipeline_mode=pl.Buffered(1))


def _dot(a, b):
    return jnp.dot(a, b, preferred_element_type=F32)


def _dot_nt(a, b):
    return lax.dot_general(a, b, (((1,), (1,)), ((), ())), preferred_element_type=F32)


def _split3(x):
    hi = x.astype(BF16)
    r = x - hi.astype(F32)
    mid = r.astype(BF16)
    lo = (r - mid.astype(F32)).astype(BF16)
    return hi, mid, lo


def _dot3_r(x, m):
    hi, mid, lo = _split3(x)
    return _dot(hi, m) + _dot(mid, m) + _dot(lo, m)


def _dot3_l(m, x):
    hi, mid, lo = _split3(x)
    return _dot(m, hi) + _dot(m, mid) + _dot(m, lo)


def _dot3_nt(m, x):
    hi, mid, lo = _split3(x)
    return _dot_nt(m, hi) + _dot_nt(m, mid) + _dot_nt(m, lo)


def _sigmoid(x):
    return 1.0 / (1.0 + jnp.exp(-x))


def _silu(x):
    return x * _sigmoid(x)


def _gelu(x):
    return 0.5 * x * (1.0 + jnp.tanh(0.7978845608028654 * (x + 0.044715 * (x * x * x))))


def _log_sigmoid(x):
    return jnp.minimum(x, 0.0) - jnp.log1p(jnp.exp(-jnp.abs(x)))


def _layer_norm(x, g, b):
    mu = jnp.mean(x, axis=-1, keepdims=True)
    d = x - mu
    var = jnp.mean(d * d, axis=-1, keepdims=True)
    return d * lax.rsqrt(var + LN_EPS) * g + b


def _lane_group_mask(rows, width, group, g):
    lane = lax.broadcasted_iota(jnp.int32, (rows, width), 1)
    return (lane >= g * group) & (lane < (g + 1) * group)


def _mixer_front(x, wmix_ref, bf_ref):
    z = _dot(x.astype(BF16), wmix_ref[...])
    seg = lambda i: z[:, i * W_BR:(i + 1) * W_BR]
    f = z[:, 10 * W_BR:10 * W_BR + LANES]
    lane = lax.broadcasted_iota(jnp.int32, f.shape, 1)
    logf = jnp.where(lane < HA, _log_sigmoid(f + bf_ref[...]), 0.0)
    return [seg(i) for i in range(10)], logf


def _gmlp(ub, vb, gavg_ref, lng_ref, lnb_ref, msp_ref, bsp_ref):
    rows = ub.shape[0]
    ug = _gelu(ub)
    vg = _gelu(vb)
    gavg = gavg_ref[...]
    mu = _dot3_r(vg, gavg)
    dv = vg - mu
    var = _dot3_r(dv * dv, gavg)
    vn = dv * lax.rsqrt(var + LN_EPS) * lng_ref[...] + lnb_ref[...]
    vnb = vn.astype(BF16)
    mix = bsp_ref[...]
    for g in range(GB):
        mg = _dot(msp_ref[g], vnb)
        mix = mix + jnp.where(_lane_group_mask(rows, W_BR, DB, g), mg, 0.0)
    return ug * mix, vn


def _k1_prompt_kernel(x_ref, wmix_ref, bf_ref, gavg_ref, lng_ref, lnb_ref, msp_ref, bsp_ref,
                      wc_ref, bc_ref, lcg_ref, lcb_ref, wd_ref, tril_ref,
                      q_ref, k_ref, v_ref, kb_ref, vb_ref, logf_ref, ct_ref,
                      yb_ref, yc_ref, yd_ref, sc_ref, sd_ref,
                      xpc, xpd, carry, xsh, *, tm):
    i = pl.program_id(1)

    @pl.when(i == 0)
    def _():
        xpc[0:HALO_C, :] = jnp.zeros((HALO_C, W_BR), F32)
        xpd[0:HALO_D, :] = jnp.zeros((HALO_D, W_BR), F32)
        carry[...] = jnp.zeros_like(carry)

    (q, k, v, ub, vb, ac, gc, bd, cd, hd), logf = _mixer_front(x_ref[0], wmix_ref, bf_ref)
    q_ref[0] = (q * Q_SCALE).astype(BF16)
    k_ref[0] = k
    v_ref[0] = v
    kb_ref[0] = k.astype(BF16)
    vb_ref[0] = v.astype(BF16)
    logf_ref[0] = logf[:, 0:HA]

    c = _dot3_l(tril_ref[...], logf) + carry[0:1, :]
    carry[0:1, :] = c[tm - 1:tm, :]
    ct_ref[0] = c.T[0:SUBLANES, :]

    yb, _ = _gmlp(ub, vb, gavg_ref, lng_ref, lnb_ref, msp_ref, bsp_ref)
    yb_ref[0] = yb.astype(BF16)

    cin = ac * _sigmoid(gc)
    xpc[HALO_C:HALO_C + tm, :] = cin
    span = tm + HALO_C - SUBLANES
    for r in range(1, SUBLANES):
        xsh[r - 1, 0:span, :] = xpc[r:r + span, :]
    acc = jnp.zeros((tm, W_BR), F32) + bc_ref[...]
    for j in range(KC):
        off = HALO_C - (KC - 1) + j
        r = off % SUBLANES
        a = off - r
        win = xpc[a:a + tm, :] if r == 0 else xsh[r - 1, a:a + tm, :]
        acc = acc + wc_ref[j:j + 1, :] * win
    xpc[0:HALO_C, :] = cin[tm - HALO_C:tm, :]
    sc_ref[0] = cin[tm - HALO_C:tm, :]
    yc_ref[0] = _silu(_layer_norm(acc, lcg_ref[...], lcb_ref[...])).astype(BF16)

    din = cd * hd
    xpd[HALO_D:HALO_D + tm, :] = din
    accd = jnp.zeros((tm, W_BR), F32)
    for j in range(KD):
        off = HALO_D - (KD - 1) + j
        accd = accd + wd_ref[j:j + 1, :] * xpd[off:off + tm, :]
    xpd[0:HALO_D, :] = din[tm - HALO_D:tm, :]
    sd_ref[0] = din[tm - HALO_D:tm, :]
    yd_ref[0] = (bd * accd).astype(BF16)


def _k1_prompt(l, x, wp, tm):
    b, s, _ = x.shape
    nt = s // tm
    row = lambda bb, i: (bb, i, 0)
    cst2 = lambda bb, i: (l, 0, 0)
    o256 = lambda dt: jax.ShapeDtypeStruct((b, s, W_BR), dt)
    in_specs = [
        pl.BlockSpec((1, tm, D_MODEL), row),
        _const_spec((None, D_MODEL, W_MIX), cst2),
        _const_spec((None, 1, LANES), cst2),
        _const_spec((W_BR, W_BR), lambda bb, i: (0, 0)),
        _const_spec((None, 1, W_BR), cst2),
        _const_spec((None, 1, W_BR), cst2),
        _const_spec((None, GB, tm, tm), lambda bb, i: (l, 0, 0, 0)),
        _const_spec((None, tm, W_BR), cst2),
        _const_spec((None, HALO_C, W_BR), cst2),
        _const_spec((None, 1, W_BR), cst2),
        _const_spec((None, 1, W_BR), cst2),
        _const_spec((None, 1, W_BR), cst2),
        _const_spec((None, HALO_D, W_BR), cst2),
        _const_spec((tm, tm), lambda bb, i: (0, 0)),
    ]
    out_shape = [o256(BF16), o256(F32), o256(F32), o256(BF16), o256(BF16),
                 jax.ShapeDtypeStruct((b, s, HA), F32),
                 jax.ShapeDtypeStruct((b, SUBLANES, s), F32),
                 o256(BF16), o256(BF16), o256(BF16),
                 jax.ShapeDtypeStruct((b, HALO_C, W_BR), F32),
                 jax.ShapeDtypeStruct((b, HALO_D, W_BR), F32)]
    blk = pl.BlockSpec((1, tm, W_BR), row)
    out_specs = [blk, blk, blk, blk, blk,
                 pl.BlockSpec((1, tm, HA), row),
                 pl.BlockSpec((1, SUBLANES, tm), lambda bb, i: (bb, 0, i)),
                 blk, blk, blk,
                 pl.BlockSpec((1, HALO_C, W_BR), lambda bb, i: (bb, 0, 0)),
                 pl.BlockSpec((1, HALO_D, W_BR), lambda bb, i: (bb, 0, 0))]
    return pl.pallas_call(
        functools.partial(_k1_prompt_kernel, tm=tm),
        grid=(b, nt), in_specs=in_specs, out_specs=out_specs, out_shape=out_shape,
        scratch_shapes=[pltpu.VMEM((tm + HALO_C, W_BR), F32), pltpu.VMEM((tm + HALO_D, W_BR), F32),
                        pltpu.VMEM((SUBLANES, LANES), F32),
                        pltpu.VMEM((SUBLANES - 1, tm + HALO_C - SUBLANES, W_BR), F32)],
        compiler_params=_cparams(2), name="k1_prompt",
    )(x, wp["wmix"], wp["bf"], wp["gavg"], wp["lng"], wp["lnb"], wp["msp_p"], wp["bsp_p"],
      wp["wc"], wp["bc"], wp["lcg"], wp["lcb"], wp["wd"], wp["tril_p"])


def _k1_sample_kernel(x_ref, wmix_ref, bf_ref, gavg_ref, lng_ref, lnb_ref, msp_ref, bsp_ref,
                      wc_ref, bc_ref, lcg_ref, lcb_ref, wd_ref, stc_ref, std_ref,
                      q_ref, k_ref, v_ref, logf_ref, vn_ref, yb_ref, yc_ref, yd_ref, nsc_ref, nsd_ref,
                      cin_s, ypre_s, din_s, ydpre_s, xs, xsd, *, nb, t):
    (q, k, v, ub, vb, ac, gc, bd, cd, hd), logf = _mixer_front(x_ref[...], wmix_ref, bf_ref)
    q_ref[...] = q * Q_SCALE
    k_ref[...] = k
    v_ref[...] = v
    logf_ref[...] = logf
    yb, vn = _gmlp(ub, vb, gavg_ref, lng_ref, lnb_ref, msp_ref, bsp_ref)
    yb_ref[...] = yb.astype(BF16)
    vn_ref[...] = vn

    cin_s[...] = ac * _sigmoid(gc)
    din_s[...] = cd * hd

    def body(bb, _):
        r0 = pl.multiple_of(bb * t, t)
        xs[0:KC - 1, :] = stc_ref[bb]
        xs[KC - 1:KC - 1 + t, :] = cin_s[pl.ds(r0, t), :]
        acc = jnp.zeros((t, W_BR), F32) + bc_ref[...]
        for j in range(KC):
            acc = acc + wc_ref[j:j + 1, :] * xs[j:j + t, :]
        ypre_s[pl.ds(r0, t), :] = acc
        nsc_ref[bb] = xs[t:t + KC - 1, :]

        xsd[0:KD - 1, :] = std_ref[bb]
        xsd[KD - 1:KD - 1 + t, :] = din_s[pl.ds(r0, t), :]
        accd = jnp.zeros((t, W_BR), F32)
        for j in range(KD):
            accd = accd + wd_ref[j:j + 1, :] * xsd[j:j + t, :]
        ydpre_s[pl.ds(r0, t), :] = accd
        nsd_ref[bb] = xsd[t:t + KD - 1, :]
        return 0

    lax.fori_loop(0, nb, body, 0)
    yc_ref[...] = _silu(_layer_norm(ypre_s[...], lcg_ref[...], lcb_ref[...])).astype(BF16)
    yd_ref[...] = (bd * ydpre_s[...]).astype(BF16)


def _k1_sample(l, x, stc, std, wp, nb, t):
    m = nb * t
    cst2 = lambda i: (l, 0, 0)
    cst4 = lambda i: (l, 0, 0, 0)
    full = lambda shp: pl.BlockSpec(shp, lambda i: (0,) * len(shp))
    in_specs = [
        full((m, D_MODEL)),
        pl.BlockSpec((None, D_MODEL, W_MIX), cst2),
        pl.BlockSpec((None, 1, LANES), cst2),
        full((W_BR, W_BR)),
        pl.BlockSpec((None, 1, W_BR), cst2),
        pl.BlockSpec((None, 1, W_BR), cst2),
        pl.BlockSpec((None, GB, m, m), cst4),
        pl.BlockSpec((None, m, W_BR), cst2),
        pl.BlockSpec((None, HALO_C, W_BR), cst2),
        pl.BlockSpec((None, 1, W_BR), cst2),
        pl.BlockSpec((None, 1, W_BR), cst2),
        pl.BlockSpec((None, 1, W_BR), cst2),
        pl.BlockSpec((None, HALO_D, W_BR), cst2),
        pl.BlockSpec((None, nb, KC - 1, W_BR), cst4),
        pl.BlockSpec((None, nb, KD - 1, W_BR), cst4),
    ]
    o = lambda dt: jax.ShapeDtypeStruct((m, W_BR), dt)
    out_shape = [o(F32), o(F32), o(F32), jax.ShapeDtypeStruct((m, LANES), F32), o(F32),
                 o(BF16), o(BF16), o(BF16),
                 jax.ShapeDtypeStruct((nb, KC - 1, W_BR), F32),
                 jax.ShapeDtypeStruct((nb, KD - 1, W_BR), F32)]
    out_specs = [full(s.shape) for s in out_shape]
    return pl.pallas_call(
        functools.partial(_k1_sample_kernel, nb=nb, t=t),
        grid=(1,), in_specs=in_specs, out_specs=out_specs, out_shape=out_shape,
        scratch_shapes=[pltpu.VMEM((m, W_BR), F32), pltpu.VMEM((m, W_BR), F32),
                        pltpu.VMEM((m, W_BR), F32), pltpu.VMEM((m, W_BR), F32),
                        pltpu.VMEM((KC - 1 + t + 2, W_BR), F32), pltpu.VMEM((2 * SUBLANES, W_BR), F32)],
        compiler_params=pltpu.CompilerParams(vmem_limit_bytes=VMEM_LIMIT), name="k1_sample",
    )(x, wp["wmix"], wp["bf"], wp["gavg"], wp["lng"], wp["lnb"], wp["msp_s"], wp["bsp_s"],
      wp["wc"], wp["bc"], wp["lcg"], wp["lcb"], wp["wd"], stc, std)


def _attn_prompt_kernel(qi_tab, ki_tab, q_ref, k_ref, v_ref, ck_ref, cq_ref, o_ref,
                        qm_s, m_s, acc_s, *, tq, tk):
    p = pl.program_id(1)
    qi = qi_tab[p]
    ki = ki_tab[p]
    last_k = ((qi + 1) * tq - 1) // tk

    @pl.when(ki == 0)
    def _():
        q = q_ref[0]
        for h in range(HA):
            qm_s[h] = jnp.where(_lane_group_mask(tq, W_BR, DH, h), q, jnp.zeros_like(q))
        m_s[...] = jnp.full(m_s.shape, -jnp.inf, F32)
        acc_s[...] = jnp.zeros(acc_s.shape, F32)

    def step(masked, nk):
        k = k_ref[0, 0:nk, :]
        v = v_ref[0, 0:nk, :]
        ck = ck_ref[0, :, 0:nk]
        cq = cq_ref[0]
        if masked:
            keep = ((ki * tk - qi * tq) + lax.broadcasted_iota(jnp.int32, (tq, nk), 1)
                    <= lax.broadcasted_iota(jnp.int32, (tq, nk), 0))
        for h in range(HA):
            bias = (cq[h:h + 1, 0:1] - ck[h:h + 1, :]) * LOG2E
            s = _dot_nt(qm_s[h], k) + bias
            if masked:
                s = jnp.where(keep, s, -jnp.inf)
            m_prev = m_s[h]
            m_new = jnp.maximum(m_prev, jnp.max(s, axis=1, keepdims=True))
            alpha = jnp.exp2(m_prev - m_new)
            pr = jnp.exp2(s - m_new).astype(BF16)
            v_h = jnp.where(_lane_group_mask(nk, W_BR, DH, h), v, jnp.ones_like(v))
            acc_s[h] = acc_s[h] * alpha + _dot(pr, v_h)
            m_s[h] = m_new

    n_visible = (qi + 1) * tq - ki * tk
    half = tk // 2

    @pl.when(n_visible >= tk + tq)
    def _():
        step(False, tk)

    @pl.when((n_visible < tk + tq) & (n_visible > half))
    def _():
        step(True, tk)

    @pl.when(n_visible <= half)
    def _():
        step(True, half)

    @pl.when(ki == last_k)
    def _():
        out = jnp.zeros((tq, W_BR), F32)
        for h in range(HA):
            acc = acc_s[h]
            den_lane = ((h + 1) % HA) * DH
            out = out + jnp.where(_lane_group_mask(tq, W_BR, DH, h), acc / acc[:, den_lane:den_lane + 1], 0.0)
        o_ref[0] = out.astype(BF16)


def _attn_prompt(q, k, v, ct, tq, tk):
    b, s, _ = q.shape
    nq = s // tq
    n_k = [((i + 1) * tq - 1) // tk + 1 for i in range(nq)]
    qi_np = np.concatenate([np.full(n_k[i], i, np.int32) for i in range(nq)])
    ki_np = np.concatenate([np.arange(n_k[i], dtype=np.int32) for i in range(nq)])
    n_pairs = int(qi_np.shape[0])
    grid_spec = pltpu.PrefetchScalarGridSpec(
        num_scalar_prefetch=2, grid=(b, n_pairs),
        in_specs=[
            pl.BlockSpec((1, tq, W_BR), lambda bb, p, qt, kt: (bb, qt[p], 0)),
            pl.BlockSpec((1, tk, W_BR), lambda bb, p, qt, kt: (bb, kt[p], 0)),
            pl.BlockSpec((1, tk, W_BR), lambda bb, p, qt, kt: (bb, kt[p], 0)),
            pl.BlockSpec((1, SUBLANES, tk), lambda bb, p, qt, kt: (bb, 0, kt[p])),
            pl.BlockSpec((1, SUBLANES, tq), lambda bb, p, qt, kt: (bb, 0, qt[p])),
        ],
        out_specs=pl.BlockSpec((1, tq, W_BR), lambda bb, p, qt, kt: (bb, qt[p], 0)),
        scratch_shapes=[pltpu.VMEM((HA, tq, W_BR), BF16), pltpu.VMEM((HA, tq, 1), F32),
                        pltpu.VMEM((HA, tq, W_BR), F32)],
    )
    return pl.pallas_call(
        functools.partial(_attn_prompt_kernel, tq=tq, tk=tk),
        grid_spec=grid_spec, out_shape=jax.ShapeDtypeStruct((b, s, W_BR), BF16),
        compiler_params=_cparams(2), name="attn_prompt",
    )(jnp.asarray(qi_np), jnp.asarray(ki_np), q, k, v, ct, ct)


def _page_cumsum_kernel(lf_ref, u_ref, o_ref):
    o_ref[...] = _dot3_r(lf_ref[...], u_ref[...])


def _page_cumsum(lf_rows, upper, pps):
    n_rows = lf_rows.shape[0]
    return pl.pallas_call(
        _page_cumsum_kernel, grid=(n_rows // (pps * SUBLANES),),
        in_specs=[pl.BlockSpec((pps * SUBLANES, PAGE), lambda i: (i, 0)),
                  _const_spec((PAGE, PAGE), lambda i: (0, 0))],
        out_specs=pl.BlockSpec((pps * SUBLANES, PAGE), lambda i: (i, 0)),
        out_shape=jax.ShapeDtypeStruct((n_rows, PAGE), F32),
        compiler_params=_cparams(1), name="page_cumsum",
    )(lf_rows, upper)


def _attn_sample_kernel(pt_ref, q_ref, kn_ref, vn_ref, lfn_ref, pmat_ref, a8_ref, ltri_ref, rsel_ref,
                        ck_hbm, cv_hbm, cc_hbm, o_ref, kbuf, vbuf, cbuf, sem, m_s, l_s, acc_s, carry_s,
                        *, l, nb, t, n_pages, pc):
    b = pl.program_id(0)
    c = pl.program_id(1)
    nc = n_pages // pc
    step_id = b * nc + c
    slot = step_id % 2
    rows = HA * t
    ng = rows // SUBLANES

    def copies(bb, cc, sl, p):
        pg = pt_ref[bb, cc * pc + p]
        return (pltpu.make_async_copy(ck_hbm.at[l, pg], kbuf.at[sl, p], sem.at[sl, 0]),
                pltpu.make_async_copy(cv_hbm.at[l, pg], vbuf.at[sl, p], sem.at[sl, 1]),
                pltpu.make_async_copy(cc_hbm.at[l, pg], cbuf.at[sl, p], sem.at[sl, 2]))

    def issue(bb, cc, sl):
        def body(p, _):
            for cp in copies(bb, cc, sl, p):
                cp.start()
            return 0
        lax.fori_loop(0, pc, body, 0)

    @pl.when(step_id == 0)
    def _():
        issue(0, 0, 0)

    @pl.when(step_id + 1 < nb * nc)
    def _():
        nxt = step_id + 1
        issue(nxt // nc, nxt % nc, 1 - slot)

    def wait_body(p, _):
        for cp in copies(b, c, slot, p):
            cp.wait()
        return 0
    lax.fori_loop(0, pc, wait_body, 0)

    @pl.when(c == 0)
    def _():
        m_s[...] = jnp.full(m_s.shape, -jnp.inf, F32)
        l_s[...] = jnp.zeros(l_s.shape, F32)
        acc_s[...] = jnp.zeros(acc_s.shape, F32)
        carry_s[...] = jnp.zeros(carry_s.shape, F32)

    q = q_ref[0]
    sub = lax.broadcasted_iota(jnp.int32, (SUBLANES, W_BR), 0)
    lane = lax.broadcasted_iota(jnp.int32, (SUBLANES, W_BR), 1)
    head_ok = (lane // DH) == (sub % HA)
    qm = []
    for g in range(t // 2):
        qa = jnp.broadcast_to(q[2 * g:2 * g + 1, :], (SUBLANES, W_BR))
        qb = jnp.broadcast_to(q[2 * g + 1:2 * g + 2, :], (SUBLANES, W_BR))
        qm.append(jnp.where(head_ok, jnp.where(sub < HA, qa, qb), 0.0))
    qm = jnp.concatenate(qm, axis=0).astype(BF16)

    def online_update(s3, pv_fn):
        m_prev = m_s[...]
        m_new = jnp.maximum(m_prev, jnp.max(s3, axis=2, keepdims=True))
        alpha = jnp.exp2(m_prev - m_new)
        pr = jnp.exp2(s3 - m_new)
        l_s[...] = alpha * l_s[...] + jnp.sum(pr, axis=2, keepdims=True)
        acc_s[...] = acc_s[...] * alpha.reshape(rows, 1) + pv_fn(pr.reshape(rows, s3.shape[2]).astype(BF16))
        m_s[...] = m_new

    c2 = cbuf[slot].reshape(pc * SUBLANES, LANES)
    off = _dot3_l(pmat_ref[...], c2)
    carry = jnp.concatenate([carry_s[:, 0:1]] * pc, axis=0)
    cf = c2 + off[:, LANES - 1:LANES] + carry
    tot = cf[(pc - 1) * SUBLANES:pc * SUBLANES, LANES - 1:LANES]
    carry_s[...] = jnp.broadcast_to(tot, carry_s.shape)
    c_past = jnp.concatenate([cf[j * SUBLANES:(j + 1) * SUBLANES, :] for j in range(pc)], axis=1)

    s_past = jnp.concatenate([_dot(qm, kbuf[slot, j].astype(BF16)) for j in range(pc)], axis=1)
    s_past = s_past.reshape(ng, SUBLANES, pc * PAGE) - (c_past * LOG2E)[None]

    def pv_past(p2):
        acc = jnp.zeros((rows, W_BR), F32)
        for j in range(pc):
            acc = acc + _dot_nt(p2[:, j * PAGE:(j + 1) * PAGE], vbuf[slot, j].astype(BF16))
        return acc

    online_update(s_past, pv_past)

    @pl.when(c == nc - 1)
    def _():
        lfp = jnp.concatenate([lfn_ref[0], jnp.zeros((LANES - t, LANES), F32)], axis=0)
        cum_new = _dot3_l(ltri_ref[...], lfp)
        c_new = _dot3_nt(a8_ref[...], cum_new) + tot
        zpad = jnp.zeros((LANES - t, W_BR), F32)
        kn = jnp.concatenate([kn_ref[0], zpad], axis=0).astype(BF16)
        vn = jnp.concatenate([vn_ref[0], zpad], axis=0).astype(BF16)
        s_new = _dot_nt(qm, kn).reshape(ng, SUBLANES, LANES) - (c_new * LOG2E)[None]
        gi = lax.broadcasted_iota(jnp.int32, s_new.shape, 0)
        si = lax.broadcasted_iota(jnp.int32, s_new.shape, 1)
        ji = lax.broadcasted_iota(jnp.int32, s_new.shape, 2)
        s_new = jnp.where(ji <= 2 * gi + si // HA, s_new, -jnp.inf)
        online_update(s_new, lambda p2: _dot(p2, vn))

        o = acc_s[...] / l_s[...].reshape(rows, 1)
        sub_r = lax.broadcasted_iota(jnp.int32, (rows, W_BR), 0)
        lane_r = lax.broadcasted_iota(jnp.int32, (rows, W_BR), 1)
        om = jnp.where((lane_r // DH) == (sub_r % HA), o, 0.0)
        o_ref[0] = _dot(rsel_ref[...], om.astype(BF16))


def _attn_sample(l, page_table, q, kn, vn, lfn, wp, ck, cv, cc, pc):
    nb, t, _ = q.shape
    n_pages = page_table.shape[1]
    rows = HA * t
    cst = lambda shp: pl.BlockSpec(shp, lambda b, c, pt: (0,) * len(shp))
    per_b = lambda w: pl.BlockSpec((1, t, w), lambda b, c, pt: (b, 0, 0))
    grid_spec = pltpu.PrefetchScalarGridSpec(
        num_scalar_prefetch=1, grid=(nb, n_pages // pc),
        in_specs=[per_b(W_BR), per_b(W_BR), per_b(W_BR), per_b(LANES),
                  cst((pc * SUBLANES, pc * SUBLANES)), cst((SUBLANES, LANES)),
                  cst((LANES, LANES)), cst((t, rows)),
                  pl.BlockSpec(memory_space=pl.ANY), pl.BlockSpec(memory_space=pl.ANY),
                  pl.BlockSpec(memory_space=pl.ANY)],
        out_specs=per_b(W_BR),
        scratch_shapes=[pltpu.VMEM((2, pc, W_BR, PAGE), F32),
                        pltpu.VMEM((2, pc, W_BR, PAGE), F32),
                        pltpu.VMEM((2, pc, SUBLANES, LANES), F32),
                        pltpu.SemaphoreType.DMA((2, 3)),
                        pltpu.VMEM((rows // SUBLANES, SUBLANES, 1), F32),
                        pltpu.VMEM((rows // SUBLANES, SUBLANES, 1), F32),
                        pltpu.VMEM((rows, W_BR), F32),
                        pltpu.VMEM((SUBLANES, LANES), F32)],
    )
    return pl.pallas_call(
        functools.partial(_attn_sample_kernel, l=l, nb=nb, t=t, n_pages=n_pages, pc=pc),
        grid_spec=grid_spec, out_shape=jax.ShapeDtypeStruct((nb, t, W_BR), F32),
        compiler_params=_cparams(2), name="attn_sample",
    )(page_table, q, kn, vn, lfn, wp["pmat"], wp["a8"], wp["ltri"], wp["rsel"], ck, cv, cc)


def _merge_kernel(x_ref, ya_ref, yb_ref, yc_ref, yd_ref, wg_ref, wbr_ref, wo_ref, g_ref, b_ref, o_ref,
                  *, alpha):
    x = x_ref[...]
    xb = x.astype(BF16)
    ys = (ya_ref, yb_ref, yc_ref, yd_ref)
    acc = jnp.zeros(x.shape, F32)
    for n in range(N_BRANCH):
        gate = _sigmoid(_dot(xb, wg_ref[:, n * D_MODEL:(n + 1) * D_MODEL]))
        acc = acc + gate * _dot(ys[n][...].astype(BF16), wbr_ref[n])
    mix = _dot(acc.astype(BF16), wo_ref[...])
    o_ref[...] = _layer_norm(alpha * x + mix, g_ref[...], b_ref[...])


def _merge(l, x, ya, yb, yc, yd, wp, tm, alpha):
    m = x.shape[0]
    row = lambda w: pl.BlockSpec((tm, w), lambda i: (i, 0))
    c3 = lambda i: (l, 0, 0)
    return pl.pallas_call(
        functools.partial(_merge_kernel, alpha=alpha),
        grid=(m // tm,),
        in_specs=[row(D_MODEL), row(W_BR), row(W_BR), row(W_BR), row(W_BR),
                  _const_spec((None, D_MODEL, N_BRANCH * D_MODEL), c3),
                  _const_spec((None, N_BRANCH, W_BR, D_MODEL), lambda i: (l, 0, 0, 0)),
                  _const_spec((None, D_MODEL, D_MODEL), c3),
                  _const_spec((None, 1, D_MODEL), c3), _const_spec((None, 1, D_MODEL), c3)],
        out_specs=row(D_MODEL), out_shape=jax.ShapeDtypeStruct((m, D_MODEL), F32),
        compiler_params=_cparams(1), name="merge",
    )(x, ya, yb, yc, yd, wp["wgate"], wp["wbr"], wp["wout"], wp["ln1g"], wp["ln1b"])


def _moe_kernel(x_ref, ple_ref, wrh_ref, wrl_ref, br_ref, wg_ref, wu_ref, wd_ref, wpg_ref, wpp_ref,
                g_ref, b_ref, o_ref, *, alpha):
    x = x_ref[...]
    tm = x.shape[0]
    xb = x.astype(BF16)
    xm = (x - xb.astype(F32)).astype(BF16)

    logits = _dot(xb, wrh_ref[...]) + _dot(xb, wrl_ref[...]) + _dot(xm, wrh_ref[...]) + br_ref[...]
    lane = lax.broadcasted_iota(jnp.int32, (tm, LANES), 1)
    big = jnp.int32(LANES)
    lg = jnp.where(lane < N_GROUPS, logits, -jnp.inf)
    mg = jnp.max(lg, axis=1, keepdims=True)
    p_top = 1.0 / jnp.sum(jnp.exp(lg - mg), axis=1, keepdims=True)
    g_idx = jnp.min(jnp.where(lg == mg, lane, big), axis=1, keepdims=True)
    lo = N_GROUPS + g_idx * EXP_PER_GROUP
    le = jnp.where((lane >= lo) & (lane < lo + EXP_PER_GROUP), logits, -jnp.inf)
    v1 = jnp.max(le, axis=1, keepdims=True)
    i1 = jnp.min(jnp.where(le == v1, lane, big), axis=1, keepdims=True)
    le2 = jnp.where(lane == i1, -jnp.inf, le)
    v2 = jnp.max(le2, axis=1, keepdims=True)
    i2 = jnp.min(jnp.where(le2 == v2, lane, big), axis=1, keepdims=True)
    e2 = jnp.exp(v2 - v1)
    w1 = p_top / (1.0 + e2)
    w2 = p_top * e2 / (1.0 + e2)
    comb = jnp.where(lane == i1, w1, 0.0) + jnp.where(lane == i2, w2, 0.0)

    acc = jnp.zeros(x.shape, F32)
    for e in range(N_EXPERTS):
        hg = _dot(xb, wg_ref[e])
        hu = _dot(xb, wu_ref[e])
        hh = _silu(hg) * hu * comb[:, N_GROUPS + e:N_GROUPS + e + 1]
        acc = acc + _dot(hh.astype(BF16), wd_ref[e])

    ple_term = _sigmoid(_dot(xb, wpg_ref[...])) * _dot(ple_ref[...].astype(BF16), wpp_ref[...])
    o_ref[...] = _layer_norm(alpha * x + acc + ple_term, g_ref[...], b_ref[...])


def _moe(l, x, ple, wp, tm, alpha):
    m = x.shape[0]
    c3 = lambda i: (l, 0, 0)
    c4 = lambda i: (l, 0, 0, 0)
    return pl.pallas_call(
        functools.partial(_moe_kernel, alpha=alpha),
        grid=(m // tm,),
        in_specs=[pl.BlockSpec((tm, D_MODEL), lambda i: (i, 0)),
                  pl.BlockSpec((None, tm, D_PLE), lambda i: (l, i, 0)),
                  _const_spec((None, D_MODEL, LANES), c3), _const_spec((None, D_MODEL, LANES), c3),
                  _const_spec((None, 1, LANES), c3),
                  _const_spec((None, N_EXPERTS, D_MODEL, D_EXPERT), c4),
                  _const_spec((None, N_EXPERTS, D_MODEL, D_EXPERT), c4),
                  _const_spec((None, N_EXPERTS, D_EXPERT, D_MODEL), c4),
                  _const_spec((None, D_MODEL, D_MODEL), c3), _const_spec((None, D_PLE, D_MODEL), c3),
                  _const_spec((None, 1, D_MODEL), c3), _const_spec((None, 1, D_MODEL), c3)],
        out_specs=pl.BlockSpec((tm, D_MODEL), lambda i: (i, 0)),
        out_shape=jax.ShapeDtypeStruct((m, D_MODEL), F32),
        compiler_params=_cparams(1), name="moe",
    )(x, ple, wp["wr_hi"], wp["wr_lo"], wp["br"], wp["weg"], wp["weu"], wp["wed"],
      wp["wpg"], wp["wpp"], wp["ln2g"], wp["ln2b"])


def _kron_tril(w_spatial, n_blocks, n):
    w = jnp.tril(w_spatial)[:, :, :n, :n]
    eye = jnp.eye(n_blocks, dtype=w.dtype)
    out = jnp.einsum("ab,lgij->lgaibj", eye, w)
    d = w_spatial.shape[0]
    return out.reshape(d, GB, n_blocks * n, n_blocks * n).astype(BF16)


def _prepare(w_in, b_forget, ln_gmlp_g, ln_gmlp_b, w_spatial, b_spatial, w_conv_c, b_conv_c, ln_conv_g,
             ln_conv_b, w_conv_d, w_branch, w_out, ln1_g, ln1_b, w_router_group, b_router_group,
             w_router_expert, b_router_expert, w_exp_gate, w_exp_up, w_exp_down, w_ple_gate, w_ple_proj,
             ln2_g, ln2_b, tm_p, nb, t, pc, pps):
    d = w_in.shape[0]
    o_f = 3 * W_BR
    o_rest = o_f + HA
    o_gate = o_rest + 7 * W_BR
    r3 = lambda a: a.reshape(d, 1, -1)
    wp = {}
    wp["wmix"] = jnp.concatenate(
        [w_in[:, :, 0:o_f], w_in[:, :, o_rest:o_gate],
         jnp.pad(w_in[:, :, o_f:o_rest], ((0, 0), (0, 0), (0, LANES - HA)))], axis=-1).astype(BF16)
    wp["wgate"] = w_in[:, :, o_gate:].astype(BF16)
    wp["bf"] = jnp.pad(b_forget, ((0, 0), (0, LANES - HA))).reshape(d, 1, LANES)
    gi = np.arange(W_BR) // DB
    wp["gavg"] = jnp.asarray((gi[:, None] == gi[None, :]).astype(np.float32) / DB, dtype=BF16)
    wp["lng"] = r3(ln_gmlp_g)
    wp["lnb"] = r3(ln_gmlp_b)
    wp["msp_p"] = _kron_tril(w_spatial, tm_p // CHUNK, CHUNK)
    wp["msp_s"] = _kron_tril(w_spatial, nb, t)
    bsp = jnp.repeat(jnp.swapaxes(b_spatial, 1, 2), DB, axis=2)
    wp["bsp_p"] = jnp.tile(bsp, (1, tm_p // CHUNK, 1))
    wp["bsp_s"] = jnp.tile(bsp[:, :t, :], (1, nb, 1))
    wp["wc"] = jnp.pad(w_conv_c, ((0, 0), (0, HALO_C - KC), (0, 0)))
    wp["bc"] = r3(b_conv_c)
    wp["lcg"] = r3(ln_conv_g)
    wp["lcb"] = r3(ln_conv_b)
    wp["wd"] = jnp.pad(w_conv_d, ((0, 0), (0, HALO_D - KD), (0, 0)))
    wp["tril_p"] = jnp.asarray(np.tril(np.ones((tm_p, tm_p), np.float32)), dtype=BF16)
    wp["wbr"] = w_branch.astype(BF16)
    wp["wout"] = w_out.astype(BF16)
    wp["ln1g"] = r3(ln1_g)
    wp["ln1b"] = r3(ln1_b)
    wr = jnp.concatenate([w_router_group, w_router_expert,
                          jnp.zeros((d, D_MODEL, LANES - N_GROUPS - N_EXPERTS), F32)], axis=-1)
    wr_hi = wr.astype(BF16)
    wp["wr_hi"] = wr_hi
    wp["wr_lo"] = (wr - wr_hi.astype(F32)).astype(BF16)
    wp["br"] = jnp.concatenate([b_router_group, b_router_expert,
                                jnp.zeros((d, LANES - N_GROUPS - N_EXPERTS), F32)], axis=-1).reshape(d, 1, LANES)
    wp["weg"] = w_exp_gate.astype(BF16)
    wp["weu"] = w_exp_up.astype(BF16)
    wp["wed"] = w_exp_down.astype(BF16)
    wp["wpg"] = w_ple_gate.astype(BF16)
    wp["wpp"] = w_ple_proj.astype(BF16)
    wp["ln2g"] = r3(ln2_g)
    wp["ln2b"] = r3(ln2_b)

    rows = pc * SUBLANES
    ri = np.arange(rows)
    pm = ((ri[:, None] % SUBLANES) == (ri[None, :] % SUBLANES)) & ((ri[None, :] // SUBLANES) < (ri[:, None] // SUBLANES))
    wp["pmat"] = jnp.asarray(pm.astype(np.float32), dtype=BF16)
    a8 = (np.arange(LANES)[None, :] == (np.arange(SUBLANES)[:, None] % HA)).astype(np.float32)
    wp["a8"] = jnp.asarray(a8, dtype=BF16)
    wp["ltri"] = jnp.asarray(np.tril(np.ones((LANES, LANES), np.float32)), dtype=BF16)
    rsel = ((np.arange(HA * t)[None, :] // HA) == np.arange(t)[:, None]).astype(np.float32)
    wp["rsel"] = jnp.asarray(rsel, dtype=BF16)
    wp["upper"] = jnp.asarray(np.triu(np.ones((PAGE, PAGE), np.float32)), dtype=BF16)
    return wp


def kernel(x_prompt, x_sample, cache_k, cache_v, cache_logf, state_conv_c, state_conv_d, page_table,
           p_prompt, p_sample, w_in, b_forget, ln_gmlp_g, ln_gmlp_b, w_spatial, b_spatial, w_conv_c,
           b_conv_c, ln_conv_g, ln_conv_b, w_conv_d, w_branch, w_out, ln1_g, ln1_b, w_router_group,
           b_router_group, w_router_expert, b_router_expert, w_exp_gate, w_exp_up, w_exp_down,
           w_ple_gate, w_ple_proj, ln2_g, ln2_b):
    depth = w_in.shape[0]
    alpha = float((2 * depth) ** 0.25)
    b, s, _ = x_prompt.shape
    nb, t, _ = x_sample.shape
    n_pool = cache_k.shape[1]
    n_pages = page_table.shape[1]
    tm_p = min(512, s)
    tq, tk = min(512, s), min(1024, s)
    ms = nb * t
    pc = min(32, n_pages)
    n_pg = depth * n_pool
    pps = next(c for c in (256, 128, 64, 32, 16, 8, 4, 2, 1) if n_pg % c == 0)

    wp = _prepare(w_in, b_forget, ln_gmlp_g, ln_gmlp_b, w_spatial, b_spatial, w_conv_c, b_conv_c,
                  ln_conv_g, ln_conv_b, w_conv_d, w_branch, w_out, ln1_g, ln1_b, w_router_group,
                  b_router_group, w_router_expert, b_router_expert, w_exp_gate, w_exp_up, w_exp_down,
                  w_ple_gate, w_ple_proj, ln2_g, ln2_b, tm_p, nb, t, pc, pps)

    ck = jnp.transpose(cache_k, (0, 1, 3, 4, 2)).reshape(depth, n_pool, W_BR, PAGE)
    cv = jnp.transpose(cache_v, (0, 1, 3, 4, 2)).reshape(depth, n_pool, W_BR, PAGE)
    lf_t = jnp.transpose(cache_logf, (0, 1, 3, 2))
    lf_rows = jnp.concatenate([lf_t, lf_t], axis=2).reshape(n_pg * SUBLANES, PAGE)
    cc = _page_cumsum(lf_rows, wp["upper"], pps).reshape(depth, n_pool, SUBLANES, LANES)

    pp = p_prompt.reshape(depth, b * s, D_PLE)
    ps = p_sample.reshape(depth, ms, D_PLE)

    xp = x_prompt
    xs = x_sample.reshape(ms, D_MODEL)
    kp, vp, lfp, ccp, cdp = [], [], [], [], []
    ks_, vs_, lfs, ccs, cds, gvs = [], [], [], [], [], []
    for l in range(depth):
        (q, k32, v32, kb, vb, logf, ct, yb, yc, yd, stc, std) = _k1_prompt(l, xp, wp, tm_p)
        ya = _attn_prompt(q, kb, vb, ct, tq, tk)
        f2 = lambda a: a.reshape(b * s, a.shape[-1])
        x1 = _merge(l, f2(xp), f2(ya), f2(yb), f2(yc), f2(yd), wp, tm_p, alpha)
        xp = _moe(l, x1, pp, wp, tm_p, alpha).reshape(b, s, D_MODEL)
        kp.append(k32.reshape(b, s, HA, DH))
        vp.append(v32.reshape(b, s, HA, DH))
        lfp.append(logf)
        ccp.append(stc[:, HALO_C - (KC - 1):, :])
        cdp.append(std[:, HALO_D - (KD - 1):, :])

        (qs, k_s, v_s, lf_s, vn_s, yb_s, yc_s, yd_s, nsc, nsd) = _k1_sample(
            l, xs, state_conv_c, state_conv_d, wp, nb, t)
        r3 = lambda a: a.reshape(nb, t, a.shape[-1])
        ya_s = _attn_sample(l, page_table, r3(qs), r3(k_s), r3(v_s), r3(lf_s), wp, ck, cv, cc, pc)
        x1s = _merge(l, xs, ya_s.reshape(ms, W_BR), yb_s, yc_s, yd_s, wp, ms, alpha)
        xs = _moe(l, x1s, ps, wp, ms, alpha)
        ks_.append(k_s.reshape(nb, t, HA, DH))
        vs_.append(v_s.reshape(nb, t, HA, DH))
        lfs.append(lf_s[:, 0:HA].reshape(nb, t, HA))
        ccs.append(nsc)
        cds.append(nsd)
        gvs.append(vn_s.reshape(nb, t, GB, DB))

    return (xp, xs.reshape(nb, t, D_MODEL), jnp.stack(kp), jnp.stack(vp), jnp.stack(lfp),
            jnp.stack(ks_), jnp.stack(vs_), jnp.stack(lfs), jnp.stack(ccp), jnp.stack(ccs),
            jnp.stack(cdp), jnp.stack(cds), jnp.stack(gvs))
```

```python
import functools

import numpy as np
import jax
import jax.numpy as jnp
from jax import lax
from jax.experimental import pallas as pl
from jax.experimental.pallas import tpu as pltpu

F32 = jnp.float32
BF16 = jnp.bfloat16

D_MODEL = 1024
W_BR = 256
HA = 4
DH = 64
GB = 4
DB = 64
CHUNK = 128
KC = 31
KD = 3
N_BRANCH = 4
N_GROUPS = 4
EXP_PER_GROUP = 4
N_EXPERTS = 16
D_EXPERT = 256
D_PLE = 256
PAGE = 128
LN_EPS = 1e-5
ATTN_SCALE = DH ** -0.5
LOG2E = 1.4426950408889634
Q_SCALE = ATTN_SCALE * LOG2E
LANES = 128
SUBLANES = 8
HALO_C = 32
HALO_D = 8
W_MIX = 10 * W_BR + LANES
VMEM_LIMIT = 56 * 1024 * 1024


def _cparams(n_axes):
    return pltpu.CompilerParams(dimension_semantics=("arbitrary",) * n_axes,
                                vmem_limit_bytes=VMEM_LIMIT)


def _const_spec(shape, index_map):
    return pl.BlockSpec(shape, index_map, pipeline_mode=pl.Buffered(1))


def _dot(a, b):
    return jnp.dot(a, b, preferred_element_type=F32)


def _dot_nt(a, b):
    return lax.dot_general(a, b, (((1,), (1,)), ((), ())), preferred_element_type=F32)


def _split3(x):
    hi = x.astype(BF16)
    r = x - hi.astype(F32)
    mid = r.astype(BF16)
    lo = (r - mid.astype(F32)).astype(BF16)
    return hi, mid, lo


def _dot3_r(x, m):
    hi, mid, lo = _split3(x)
    return _dot(hi, m) + _dot(mid, m) + _dot(lo, m)


def _dot3_l(m, x):
    hi, mid, lo = _split3(x)
    return _dot(m, hi) + _dot(m, mid) + _dot(m, lo)


def _dot3_nt(m, x):
    hi, mid, lo = _split3(x)
    return _dot_nt(m, hi) + _dot_nt(m, mid) + _dot_nt(m, lo)


def _sigmoid(x):
    return 1.0 / (1.0 + jnp.exp(-x))


def _silu(x):
    return x * _sigmoid(x)


def _gelu(x):
    return 0.5 * x * (1.0 + jnp.tanh(0.7978845608028654 * (x + 0.044715 * (x * x * x))))


def _log_sigmoid(x):
    return jnp.minimum(x, 0.0) - jnp.log1p(jnp.exp(-jnp.abs(x)))


def _layer_norm(x, g, b):
    mu = jnp.mean(x, axis=-1, keepdims=True)
    d = x - mu
    var = jnp.mean(d * d, axis=-1, keepdims=True)
    return d * lax.rsqrt(var + LN_EPS) * g + b


def _lane_group_mask(rows, width, group, g):
    lane = lax.broadcasted_iota(jnp.int32, (rows, width), 1)
    return (lane >= g * group) & (lane < (g + 1) * group)


def _mixer_front(x, wmix_ref, bf_ref):
    z = _dot(x.astype(BF16), wmix_ref[...])
    seg = lambda i: z[:, i * W_BR:(i + 1) * W_BR]
    f = z[:, 10 * W_BR:10 * W_BR + LANES]
    lane = lax.broadcasted_iota(jnp.int32, f.shape, 1)
    logf = jnp.where(lane < HA, _log_sigmoid(f + bf_ref[...]), 0.0)
    return [seg(i) for i in range(10)], logf


def _gmlp(ub, vb, gavg_ref, lng_ref, lnb_ref, msp_ref, bsp_ref):
    rows = ub.shape[0]
    ug = _gelu(ub)
    vg = _gelu(vb)
    gavg = gavg_ref[...]
    mu = _dot3_r(vg, gavg)
    dv = vg - mu
    var = _dot3_r(dv * dv, gavg)
    vn = dv * lax.rsqrt(var + LN_EPS) * lng_ref[...] + lnb_ref[...]
    vnb = vn.astype(BF16)
    mix = bsp_ref[...]
    for g in range(GB):
        mg = _dot(msp_ref[g], vnb)
        mix = mix + jnp.where(_lane_group_mask(rows, W_BR, DB, g), mg, 0.0)
    return ug * mix, vn


def _k1_prompt_kernel(x_ref, wmix_ref, bf_ref, gavg_ref, lng_ref, lnb_ref, msp_ref, bsp_ref,
                      wc_ref, bc_ref, lcg_ref, lcb_ref, wd_ref, tril_ref,
                      q_ref, k_ref, v_ref, kb_ref, vb_ref, logf_ref, ct_ref,
                      yb_ref, yc_ref, yd_ref, sc_ref, sd_ref,
                      xpc, xpd, carry, xsh, *, tm):
    i = pl.program_id(1)

    @pl.when(i == 0)
    def _():
        xpc[0:HALO_C, :] = jnp.zeros((HALO_C, W_BR), F32)
        xpd[0:HALO_D, :] = jnp.zeros((HALO_D, W_BR), F32)
        carry[...] = jnp.zeros_like(carry)

    (q, k, v, ub, vb, ac, gc, bd, cd, hd), logf = _mixer_front(x_ref[0], wmix_ref, bf_ref)
    q_ref[0] = (q * Q_SCALE).astype(BF16)
    k_ref[0] = k
    v_ref[0] = v
    kb_ref[0] = k.astype(BF16)
    vb_ref[0] = v.astype(BF16)
    logf_ref[0] = logf[:, 0:HA]

    c = _dot3_l(tril_ref[...], logf) + carry[0:1, :]
    carry[0:1, :] = c[tm - 1:tm, :]
    ct_ref[0] = c.T[0:SUBLANES, :]

    yb, _ = _gmlp(ub, vb, gavg_ref, lng_ref, lnb_ref, msp_ref, bsp_ref)
    yb_ref[0] = yb.astype(BF16)

    cin = ac * _sigmoid(gc)
    xpc[HALO_C:HALO_C + tm, :] = cin
    span = tm + HALO_C - SUBLANES
    for r in range(1, SUBLANES):
        xsh[r - 1, 0:span, :] = xpc[r:r + span, :]
    acc = jnp.zeros((tm, W_BR), F32) + bc_ref[...]
    for j in range(KC):
        off = HALO_C - (KC - 1) + j
        r = off % SUBLANES
        a = off - r
        win = xpc[a:a + tm, :] if r == 0 else xsh[r - 1, a:a + tm, :]
        acc = acc + wc_ref[j:j + 1, :] * win
    xpc[0:HALO_C, :] = cin[tm - HALO_C:tm, :]
    sc_ref[0] = cin[tm - HALO_C:tm, :]
    yc_ref[0] = _silu(_layer_norm(acc, lcg_ref[...], lcb_ref[...])).astype(BF16)

    din = cd * hd
    xpd[HALO_D:HALO_D + tm, :] = din
    accd = jnp.zeros((tm, W_BR), F32)
    for j in range(KD):
        off = HALO_D - (KD - 1) + j
        accd = accd + wd_ref[j:j + 1, :] * xpd[off:off + tm, :]
    xpd[0:HALO_D, :] = din[tm - HALO_D:tm, :]
    sd_ref[0] = din[tm - HALO_D:tm, :]
    yd_ref[0] = (bd * accd).astype(BF16)


def _k1_prompt(l, x, wp, tm):
    b, s, _ = x.shape
    nt = s // tm
    row = lambda bb, i: (bb, i, 0)
    cst2 = lambda bb, i: (l, 0, 0)
    o256 = lambda dt: jax.ShapeDtypeStruct((b, s, W_BR), dt)
    in_specs = [
        pl.BlockSpec((1, tm, D_MODEL), row),
        _const_spec((None, D_MODEL, W_MIX), cst2),
        _const_spec((None, 1, LANES), cst2),
        _const_spec((W_BR, W_BR), lambda bb, i: (0, 0)),
        _const_spec((None, 1, W_BR), cst2),
        _const_spec((None, 1, W_BR), cst2),
        _const_spec((None, GB, tm, tm), lambda bb, i: (l, 0, 0, 0)),
        _const_spec((None, tm, W_BR), cst2),
        _const_spec((None, HALO_C, W_BR), cst2),
        _const_spec((None, 1, W_BR), cst2),
        _const_spec((None, 1, W_BR), cst2),
        _const_spec((None, 1, W_BR), cst2),
        _const_spec((None, HALO_D, W_BR), cst2),
        _const_spec((tm, tm), lambda bb, i: (0, 0)),
    ]
    out_shape = [o256(BF16), o256(F32), o256(F32), o256(BF16), o256(BF16),
                 jax.ShapeDtypeStruct((b, s, HA), F32),
                 jax.ShapeDtypeStruct((b, SUBLANES, s), F32),
                 o256(BF16), o256(BF16), o256(BF16),
                 jax.ShapeDtypeStruct((b, HALO_C, W_BR), F32),
                 jax.ShapeDtypeStruct((b, HALO_D, W_BR), F32)]
    blk = pl.BlockSpec((1, tm, W_BR), row)
    out_specs = [blk, blk, blk, blk, blk,
                 pl.BlockSpec((1, tm, HA), row),
                 pl.BlockSpec((1, SUBLANES, tm), lambda bb, i: (bb, 0, i)),
                 blk, blk, blk,
                 pl.BlockSpec((1, HALO_C, W_BR), lambda bb, i: (bb, 0, 0)),
                 pl.BlockSpec((1, HALO_D, W_BR), lambda bb, i: (bb, 0, 0))]
    return pl.pallas_call(
        functools.partial(_k1_prompt_kernel, tm=tm),
        grid=(b, nt), in_specs=in_specs, out_specs=out_specs, out_shape=out_shape,
        scratch_shapes=[pltpu.VMEM((tm + HALO_C, W_BR), F32), pltpu.VMEM((tm + HALO_D, W_BR), F32),
                        pltpu.VMEM((SUBLANES, LANES), F32),
                        pltpu.VMEM((SUBLANES - 1, tm + HALO_C - SUBLANES, W_BR), F32)],
        compiler_params=_cparams(2), name="k1_prompt",
    )(x, wp["wmix"], wp["bf"], wp["gavg"], wp["lng"], wp["lnb"], wp["msp_p"], wp["bsp_p"],
      wp["wc"], wp["bc"], wp["lcg"], wp["lcb"], wp["wd"], wp["tril_p"])


def _k1_sample_kernel(x_ref, wmix_ref, bf_ref, gavg_ref, lng_ref, lnb_ref, msp_ref, bsp_ref,
                      wc_ref, bc_ref, lcg_ref, lcb_ref, wd_ref, stc_ref, std_ref,
                      q_ref, k_ref, v_ref, logf_ref, vn_ref, yb_ref, yc_ref, yd_ref, nsc_ref, nsd_ref,
                      cin_s, ypre_s, din_s, ydpre_s, xs, xsd, *, nb, t):
    (q, k, v, ub, vb, ac, gc, bd, cd, hd), logf = _mixer_front(x_ref[...], wmix_ref, bf_ref)
    q_ref[...] = q * Q_SCALE
    k_ref[...] = k
    v_ref[...] = v
    logf_ref[...] = logf
    yb, vn = _gmlp(ub, vb, gavg_ref, lng_ref, lnb_ref, msp_ref, bsp_ref)
    yb_ref[...] = yb.astype(BF16)
    vn_ref[...] = vn

    cin_s[...] = ac * _sigmoid(gc)
    din_s[...] = cd * hd

    def body(bb, _):
        r0 = pl.multiple_of(bb * t, t)
        xs[0:KC - 1, :] = stc_ref[bb]
        xs[KC - 1:KC - 1 + t, :] = cin_s[pl.ds(r0, t), :]
        acc = jnp.zeros((t, W_BR), F32) + bc_ref[...]
        for j in range(KC):
            acc = acc + wc_ref[j:j + 1, :] * xs[j:j + t, :]
        ypre_s[pl.ds(r0, t), :] = acc
        nsc_ref[bb] = xs[t:t + KC - 1, :]

        xsd[0:KD - 1, :] = std_ref[bb]
        xsd[KD - 1:KD - 1 + t, :] = din_s[pl.ds(r0, t), :]
        accd = jnp.zeros((t, W_BR), F32)
        for j in range(KD):
            accd = accd + wd_ref[j:j + 1, :] * xsd[j:j + t, :]
        ydpre_s[pl.ds(r0, t), :] = accd
        nsd_ref[bb] = xsd[t:t + KD - 1, :]
        return 0

    lax.fori_loop(0, nb, body, 0)
    yc_ref[...] = _silu(_layer_norm(ypre_s[...], lcg_ref[...], lcb_ref[...])).astype(BF16)
    yd_ref[...] = (bd * ydpre_s[...]).astype(BF16)


def _k1_sample(l, x, stc, std, wp, nb, t):
    m = nb * t
    cst2 = lambda i: (l, 0, 0)
    cst4 = lambda i: (l, 0, 0, 0)
    full = lambda shp: pl.BlockSpec(shp, lambda i: (0,) * len(shp))
    in_specs = [
        full((m, D_MODEL)),
        pl.BlockSpec((None, D_MODEL, W_MIX), cst2),
        pl.BlockSpec((None, 1, LANES), cst2),
        full((W_BR, W_BR)),
        pl.BlockSpec((None, 1, W_BR), cst2),
        pl.BlockSpec((None, 1, W_BR), cst2),
        pl.BlockSpec((None, GB, m, m), cst4),
        pl.BlockSpec((None, m, W_BR), cst2),
        pl.BlockSpec((None, HALO_C, W_BR), cst2),
        pl.BlockSpec((None, 1, W_BR), cst2),
        pl.BlockSpec((None, 1, W_BR), cst2),
        pl.BlockSpec((None, 1, W_BR), cst2),
        pl.BlockSpec((None, HALO_D, W_BR), cst2),
        pl.BlockSpec((None, nb, KC - 1, W_BR), cst4),
        pl.BlockSpec((None, nb, KD - 1, W_BR), cst4),
    ]
    o = lambda dt: jax.ShapeDtypeStruct((m, W_BR), dt)
    out_shape = [o(F32), o(F32), o(F32), jax.ShapeDtypeStruct((m, LANES), F32), o(F32),
                 o(BF16), o(BF16), o(BF16),
                 jax.ShapeDtypeStruct((nb, KC - 1, W_BR), F32),
                 jax.ShapeDtypeStruct((nb, KD - 1, W_BR), F32)]
    out_specs = [full(s.shape) for s in out_shape]
    return pl.pallas_call(
        functools.partial(_k1_sample_kernel, nb=nb, t=t),
        grid=(1,), in_specs=in_specs, out_specs=out_specs, out_shape=out_shape,
        scratch_shapes=[pltpu.VMEM((m, W_BR), F32), pltpu.VMEM((m, W_BR), F32),
                        pltpu.VMEM((m, W_BR), F32), pltpu.VMEM((m, W_BR), F32),
                        pltpu.VMEM((KC - 1 + t + 2, W_BR), F32), pltpu.VMEM((2 * SUBLANES, W_BR), F32)],
        compiler_params=pltpu.CompilerParams(vmem_limit_bytes=VMEM_LIMIT), name="k1_sample",
    )(x, wp["wmix"], wp["bf"], wp["gavg"], wp["lng"], wp["lnb"], wp["msp_s"], wp["bsp_s"],
      wp["wc"], wp["bc"], wp["lcg"], wp["lcb"], wp["wd"], stc, std)


def _attn_prompt_kernel(qi_tab, ki_tab, q_ref, k_ref, v_ref, ck_ref, cq_ref, o_ref,
                        qm_s, m_s, acc_s, *, tq, tk):
    p = pl.program_id(1)
    qi = qi_tab[p]
    ki = ki_tab[p]
    last_k = ((qi + 1) * tq - 1) // tk

    @pl.when(ki == 0)
    def _():
        q = q_ref[0]
        for h in range(HA):
            qm_s[h] = jnp.where(_lane_group_mask(tq, W_BR, DH, h), q, jnp.zeros_like(q))
        m_s[...] = jnp.full(m_s.shape, -jnp.inf, F32)
        acc_s[...] = jnp.zeros(acc_s.shape, F32)

    def step(masked, nk):
        k = k_ref[0, 0:nk, :]
        v = v_ref[0, 0:nk, :]
        ck = ck_ref[0, :, 0:nk]
        cq = cq_ref[0]
        if masked:
            keep = ((ki * tk - qi * tq) + lax.broadcasted_iota(jnp.int32, (tq, nk), 1)
                    <= lax.broadcasted_iota(jnp.int32, (tq, nk), 0))
        for h in range(HA):
            bias = (cq[h:h + 1, 0:1] - ck[h:h + 1, :]) * LOG2E
            s = _dot_nt(qm_s[h], k) + bias
            if masked:
                s = jnp.where(keep, s, -jnp.inf)
            m_prev = m_s[h]
            m_new = jnp.maximum(m_prev, jnp.max(s, axis=1, keepdims=True))
            alpha = jnp.exp2(m_prev - m_new)
            pr = jnp.exp2(s - m_new).astype(BF16)
            v_h = jnp.where(_lane_group_mask(nk, W_BR, DH, h), v, jnp.ones_like(v))
            acc_s[h] = acc_s[h] * alpha + _dot(pr, v_h)
            m_s[h] = m_new

    n_visible = (qi + 1) * tq - ki * tk
    half = tk // 2

    @pl.when(n_visible >= tk + tq)
    def _():
        step(False, tk)

    @pl.when((n_visible < tk + tq) & (n_visible > half))
    def _():
        step(True, tk)

    @pl.when(n_visible <= half)
    def _():
        step(True, half)

    @pl.when(ki == last_k)
    def _():
        out = jnp.zeros((tq, W_BR), F32)
        for h in range(HA):
            acc = acc_s[h]
            den_lane = ((h + 1) % HA) * DH
            out = out + jnp.where(_lane_group_mask(tq, W_BR, DH, h), acc / acc[:, den_lane:den_lane + 1], 0.0)
        o_ref[0] = out.astype(BF16)


def _attn_prompt(q, k, v, ct, tq, tk):
    b, s, _ = q.shape
    nq = s // tq
    n_k = [((i + 1) * tq - 1) // tk + 1 for i in range(nq)]
    qi_np = np.concatenate([np.full(n_k[i], i, np.int32) for i in range(nq)])
    ki_np = np.concatenate([np.arange(n_k[i], dtype=np.int32) for i in range(nq)])
    n_pairs = int(qi_np.shape[0])
    grid_spec = pltpu.PrefetchScalarGridSpec(
        num_scalar_prefetch=2, grid=(b, n_pairs),
        in_specs=[
            pl.BlockSpec((1, tq, W_BR), lambda bb, p, qt, kt: (bb, qt[p], 0)),
            pl.BlockSpec((1, tk, W_BR), lambda bb, p, qt, kt: (bb, kt[p], 0)),
            pl.BlockSpec((1, tk, W_BR), lambda bb, p, qt, kt: (bb, kt[p], 0)),
            pl.BlockSpec((1, SUBLANES, tk), lambda bb, p, qt, kt: (bb, 0, kt[p])),
            pl.BlockSpec((1, SUBLANES, tq), lambda bb, p, qt, kt: (bb, 0, qt[p])),
        ],
        out_specs=pl.BlockSpec((1, tq, W_BR), lambda bb, p, qt, kt: (bb, qt[p], 0)),
        scratch_shapes=[pltpu.VMEM((HA, tq, W_BR), BF16), pltpu.VMEM((HA, tq, 1), F32),
                        pltpu.VMEM((HA, tq, W_BR), F32)],
    )
    return pl.pallas_call(
        functools.partial(_attn_prompt_kernel, tq=tq, tk=tk),
        grid_spec=grid_spec, out_shape=jax.ShapeDtypeStruct((b, s, W_BR), BF16),
        compiler_params=_cparams(2), name="attn_prompt",
    )(jnp.asarray(qi_np), jnp.asarray(ki_np), q, k, v, ct, ct)


def _page_cumsum_kernel(lf_ref, u_ref, o_ref):
    o_ref[...] = _dot3_r(lf_ref[...], u_ref[...])


def _page_cumsum(lf_rows, upper, pps):
    n_rows = lf_rows.shape[0]
    return pl.pallas_call(
        _page_cumsum_kernel, grid=(n_rows // (pps * SUBLANES),),
        in_specs=[pl.BlockSpec((pps * SUBLANES, PAGE), lambda i: (i, 0)),
                  _const_spec((PAGE, PAGE), lambda i: (0, 0))],
        out_specs=pl.BlockSpec((pps * SUBLANES, PAGE), lambda i: (i, 0)),
        out_shape=jax.ShapeDtypeStruct((n_rows, PAGE), F32),
        compiler_params=_cparams(1), name="page_cumsum",
    )(lf_rows, upper)


def _attn_sample_kernel(pt_ref, q_ref, kn_ref, vn_ref, lfn_ref, pmat_ref, a8_ref, ltri_ref, rsel_ref,
                        ck_hbm, cv_hbm, cc_hbm, o_ref, kbuf, vbuf, cbuf, sem, m_s, l_s, acc_s, carry_s,
                        *, l, nb, t, n_pages, pc):
    b = pl.program_id(0)
    c = pl.program_id(1)
    nc = n_pages // pc
    step_id = b * nc + c
    slot = step_id % 2
    rows = HA * t
    ng = rows // SUBLANES

    def copies(bb, cc, sl, p):
        pg = pt_ref[bb, cc * pc + p]
        return (pltpu.make_async_copy(ck_hbm.at[l, pg], kbuf.at[sl, p], sem.at[sl, 0]),
                pltpu.make_async_copy(cv_hbm.at[l, pg], vbuf.at[sl, p], sem.at[sl, 1]),
                pltpu.make_async_copy(cc_hbm.at[l, pg], cbuf.at[sl, p], sem.at[sl, 2]))

    def issue(bb, cc, sl):
        def body(p, _):
            for cp in copies(bb, cc, sl, p):
                cp.start()
            return 0
        lax.fori_loop(0, pc, body, 0)

    @pl.when(step_id == 0)
    def _():
        issue(0, 0, 0)

    @pl.when(step_id + 1 < nb * nc)
    def _():
        nxt = step_id + 1
        issue(nxt // nc, nxt % nc, 1 - slot)

    def wait_body(p, _):
        for cp in copies(b, c, slot, p):
            cp.wait()
        return 0
    lax.fori_loop(0, pc, wait_body, 0)

    @pl.when(c == 0)
    def _():
        m_s[...] = jnp.full(m_s.shape, -jnp.inf, F32)
        l_s[...] = jnp.zeros(l_s.shape, F32)
        acc_s[...] = jnp.zeros(acc_s.shape, F32)
        carry_s[...] = jnp.zeros(carry_s.shape, F32)

    q = q_ref[0]
    sub = lax.broadcasted_iota(jnp.int32, (SUBLANES, W_BR), 0)
    lane = lax.broadcasted_iota(jnp.int32, (SUBLANES, W_BR), 1)
    head_ok = (lane // DH) == (sub % HA)
    qm = []
    for g in range(t // 2):
        qa = jnp.broadcast_to(q[2 * g:2 * g + 1, :], (SUBLANES, W_BR))
        qb = jnp.broadcast_to(q[2 * g + 1:2 * g + 2, :], (SUBLANES, W_BR))
        qm.append(jnp.where(head_ok, jnp.where(sub < HA, qa, qb), 0.0))
    qm = jnp.concatenate(qm, axis=0).astype(BF16)

    def online_update(s3, pv_fn):
        m_prev = m_s[...]
        m_new = jnp.maximum(m_prev, jnp.max(s3, axis=2, keepdims=True))
        alpha = jnp.exp2(m_prev - m_new)
        pr = jnp.exp2(s3 - m_new)
        l_s[...] = alpha * l_s[...] + jnp.sum(pr, axis=2, keepdims=True)
        acc_s[...] = acc_s[...] * alpha.reshape(rows, 1) + pv_fn(pr.reshape(rows, s3.shape[2]).astype(BF16))
        m_s[...] = m_new

    c2 = cbuf[slot].reshape(pc * SUBLANES, LANES)
    off = _dot3_l(pmat_ref[...], c2)
    carry = jnp.concatenate([carry_s[:, 0:1]] * pc, axis=0)
    cf = c2 + off[:, LANES - 1:LANES] + carry
    tot = cf[(pc - 1) * SUBLANES:pc * SUBLANES, LANES - 1:LANES]
    carry_s[...] = jnp.broadcast_to(tot, carry_s.shape)
    c_past = jnp.concatenate([cf[j * SUBLANES:(j + 1) * SUBLANES, :] for j in range(pc)], axis=1)

    s_past = jnp.concatenate([_dot(qm, kbuf[slot, j].astype(BF16)) for j in range(pc)], axis=1)
    s_past = s_past.reshape(ng, SUBLANES, pc * PAGE) - (c_past * LOG2E)[None]

    def pv_past(p2):
        acc = jnp.zeros((rows, W_BR), F32)
        for j in range(pc):
            acc = acc + _dot_nt(p2[:, j * PAGE:(j + 1) * PAGE], vbuf[slot, j].astype(BF16))
        return acc

    online_update(s_past, pv_past)

    @pl.when(c == nc - 1)
    def _():
        lfp = jnp.concatenate([lfn_ref[0], jnp.zeros((LANES - t, LANES), F32)], axis=0)
        cum_new = _dot3_l(ltri_ref[...], lfp)
        c_new = _dot3_nt(a8_ref[...], cum_new) + tot
        zpad = jnp.zeros((LANES - t, W_BR), F32)
        kn = jnp.concatenate([kn_ref[0], zpad], axis=0).astype(BF16)
        vn = jnp.concatenate([vn_ref[0], zpad], axis=0).astype(BF16)
        s_new = _dot_nt(qm, kn).reshape(ng, SUBLANES, LANES) - (c_new * LOG2E)[None]
        gi = lax.broadcasted_iota(jnp.int32, s_new.shape, 0)
        si = lax.broadcasted_iota(jnp.int32, s_new.shape, 1)
        ji = lax.broadcasted_iota(jnp.int32, s_new.shape, 2)
        s_new = jnp.where(ji <= 2 * gi + si // HA, s_new, -jnp.inf)
        online_update(s_new, lambda p2: _dot(p2, vn))

        o = acc_s[...] / l_s[...].reshape(rows, 1)
        sub_r = lax.broadcasted_iota(jnp.int32, (rows, W_BR), 0)
        lane_r = lax.broadcasted_iota(jnp.int32, (rows, W_BR), 1)
        om = jnp.where((lane_r // DH) == (sub_r % HA), o, 0.0)
        o_ref[0] = _dot(rsel_ref[...], om.astype(BF16))


def _attn_sample(l, page_table, q, kn, vn, lfn, wp, ck, cv, cc, pc):
    nb, t, _ = q.shape
    n_pages = page_table.shape[1]
    rows = HA * t
    cst = lambda shp: pl.BlockSpec(shp, lambda b, c, pt: (0,) * len(shp))
    per_b = lambda w: pl.BlockSpec((1, t, w), lambda b, c, pt: (b, 0, 0))
    grid_spec = pltpu.PrefetchScalarGridSpec(
        num_scalar_prefetch=1, grid=(nb, n_pages // pc),
        in_specs=[per_b(W_BR), per_b(W_BR), per_b(W_BR), per_b(LANES),
                  cst((pc * SUBLANES, pc * SUBLANES)), cst((SUBLANES, LANES)),
                  cst((LANES, LANES)), cst((t, rows)),
                  pl.BlockSpec(memory_space=pl.ANY), pl.BlockSpec(memory_space=pl.ANY),
                  pl.BlockSpec(memory_space=pl.ANY)],
        out_specs=per_b(W_BR),
        scratch_shapes=[pltpu.VMEM((2, pc, W_BR, PAGE), F32),
                        pltpu.VMEM((2, pc, W_BR, PAGE), F32),
                        pltpu.VMEM((2, pc, SUBLANES, LANES), F32),
                        pltpu.SemaphoreType.DMA((2, 3)),
                        pltpu.VMEM((rows // SUBLANES, SUBLANES, 1), F32),
                        pltpu.VMEM((rows // SUBLANES, SUBLANES, 1), F32),
                        pltpu.VMEM((rows, W_BR), F32),
                        pltpu.VMEM((SUBLANES, LANES), F32)],
    )
    return pl.pallas_call(
        functools.partial(_attn_sample_kernel, l=l, nb=nb, t=t, n_pages=n_pages, pc=pc),
        grid_spec=grid_spec, out_shape=jax.ShapeDtypeStruct((nb, t, W_BR), F32),
        compiler_params=_cparams(2), name="attn_sample",
    )(page_table, q, kn, vn, lfn, wp["pmat"], wp["a8"], wp["ltri"], wp["rsel"], ck, cv, cc)


def _merge_kernel(x_ref, ya_ref, yb_ref, yc_ref, yd_ref, wg_ref, wbr_ref, wo_ref, g_ref, b_ref, o_ref,
                  *, alpha):
    x = x_ref[...]
    xb = x.astype(BF16)
    ys = (ya_ref, yb_ref, yc_ref, yd_ref)
    acc = jnp.zeros(x.shape, F32)
    for n in range(N_BRANCH):
        gate = _sigmoid(_dot(xb, wg_ref[:, n * D_MODEL:(n + 1) * D_MODEL]))
        acc = acc + gate * _dot(ys[n][...].astype(BF16), wbr_ref[n])
    mix = _dot(acc.astype(BF16), wo_ref[...])
    o_ref[...] = _layer_norm(alpha * x + mix, g_ref[...], b_ref[...])


def _merge(l, x, ya, yb, yc, yd, wp, tm, alpha):
    m = x.shape[0]
    row = lambda w: pl.BlockSpec((tm, w), lambda i: (i, 0))
    c3 = lambda i: (l, 0, 0)
    return pl.pallas_call(
        functools.partial(_merge_kernel, alpha=alpha),
        grid=(m // tm,),
        in_specs=[row(D_MODEL), row(W_BR), row(W_BR), row(W_BR), row(W_BR),
                  _const_spec((None, D_MODEL, N_BRANCH * D_MODEL), c3),
                  _const_spec((None, N_BRANCH, W_BR, D_MODEL), lambda i: (l, 0, 0, 0)),
                  _const_spec((None, D_MODEL, D_MODEL), c3),
                  _const_spec((None, 1, D_MODEL), c3), _const_spec((None, 1, D_MODEL), c3)],
        out_specs=row(D_MODEL), out_shape=jax.ShapeDtypeStruct((m, D_MODEL), F32),
        compiler_params=_cparams(1), name="merge",
    )(x, ya, yb, yc, yd, wp["wgate"], wp["wbr"], wp["wout"], wp["ln1g"], wp["ln1b"])


def _route_tokens_on_lanes(lt):
    sub = lax.broadcasted_iota(jnp.int32, lt.shape, 0)
    big = jnp.int32(LANES)
    lg = jnp.where(sub < N_GROUPS, lt, -jnp.inf)
    mg = jnp.max(lg, axis=0, keepdims=True)
    p_top = 1.0 / jnp.sum(jnp.exp(lg - mg), axis=0, keepdims=True)
    g_idx = jnp.min(jnp.where(lg == mg, sub, big), axis=0, keepdims=True)
    lo = N_GROUPS + g_idx * EXP_PER_GROUP
    le = jnp.where((sub >= lo) & (sub < lo + EXP_PER_GROUP), lt, -jnp.inf)
    v1 = jnp.max(le, axis=0, keepdims=True)
    i1 = jnp.min(jnp.where(le == v1, sub, big), axis=0, keepdims=True)
    le2 = jnp.where(sub == i1, -jnp.inf, le)
    v2 = jnp.max(le2, axis=0, keepdims=True)
    i2 = jnp.min(jnp.where(le2 == v2, sub, big), axis=0, keepdims=True)
    e2 = jnp.exp(v2 - v1)
    w1 = p_top / (1.0 + e2)
    w2 = p_top * e2 / (1.0 + e2)
    comb_t = jnp.where(sub == i1, w1, 0.0) + jnp.where(sub == i2, w2, 0.0)
    return g_idx, comb_t


def _moe_kernel(x_ref, ple_ref, wrh_ref, wrl_ref, br_ref, triu_ref, wg_ref, wu_ref, wd_ref, wpg_ref, wpp_ref,
                g_ref, b_ref, o_ref, moe_s, *, alpha, cap):
    x = x_ref[...]
    tm = x.shape[0]
    xb = x.astype(BF16)
    xm = (x - xb.astype(F32)).astype(BF16)
    nr = 3 * SUBLANES
    kpad = -(-cap // LANES) * LANES

    wrh = wrh_ref[...]
    lt = (_dot_nt(wrh, xb) + _dot_nt(wrl_ref[...], xb) + _dot_nt(wrh, xm) + br_ref[...])[0:nr]
    g_idx, comb_t = _route_tokens_on_lanes(lt)

    sub16 = lax.broadcasted_iota(jnp.int32, (2 * SUBLANES, tm), 0)
    onehot = jnp.where(sub16 == g_idx, 1.0, 0.0)
    before = _dot(onehot.astype(BF16), triu_ref[...])
    rank = jnp.sum(onehot * before, axis=0, keepdims=True)
    n_max = jnp.max(before[:, tm - 1:tm] + onehot[:, tm - 1:tm])

    sub8 = lax.broadcasted_iota(jnp.int32, (SUBLANES, tm), 0)
    extra = jnp.where(sub8 == 0, rank, jnp.where(sub8 == 1, g_idx.astype(F32), 0.0))
    col = jnp.concatenate([comb_t, extra, jnp.zeros((LANES - nr - SUBLANES, tm), F32)], axis=0).T
    rank_c = col[:, nr:nr + 1].astype(jnp.int32)
    g_c = col[:, nr + 1:nr + 2].astype(jnp.int32)

    def experts(rows_b, comb_rows, e_list):
        y = jnp.zeros((rows_b.shape[0], D_MODEL), F32)
        for e in e_list:
            hg = _dot(rows_b, wg_ref[e])
            hu = _dot(rows_b, wu_ref[e])
            hh = _silu(hg) * hu * comb_rows[:, N_GROUPS + e:N_GROUPS + e + 1]
            y = y + _dot(hh.astype(BF16), wd_ref[e])
        return y

    @pl.when(n_max <= cap)
    def _():
        comb_hi = col.astype(BF16)
        comb_lo = (col - comb_hi.astype(F32)).astype(BF16)
        r_iota = lax.broadcasted_iota(jnp.int32, (cap, tm), 0)
        c_iota = lax.broadcasted_iota(jnp.int32, (tm, kpad), 1)
        rank_i = rank.astype(jnp.int32)
        acc = jnp.zeros((tm, D_MODEL), F32)
        for g in range(N_GROUPS):
            pg = jnp.where((r_iota == rank_i) & (g_idx == g), 1.0, 0.0).astype(BF16)
            pgt = jnp.where((c_iota == rank_c) & (g_c == g), 1.0, 0.0).astype(BF16)
            xs = _dot(pg, xb).astype(BF16)
            cs = _dot(pg, comb_hi) + _dot(pg, comb_lo)
            y = experts(xs, cs, range(g * EXP_PER_GROUP, (g + 1) * EXP_PER_GROUP))
            if kpad > cap:
                y = jnp.concatenate([y, jnp.zeros((kpad - cap, D_MODEL), F32)], axis=0)
            acc = acc + _dot(pgt, y.astype(BF16))
        moe_s[...] = acc

    @pl.when(n_max > cap)
    def _():
        moe_s[...] = experts(xb, col, range(N_EXPERTS))

    ple_term = _sigmoid(_dot(xb, wpg_ref[...])) * _dot(ple_ref[...].astype(BF16), wpp_ref[...])
    o_ref[...] = _layer_norm(alpha * x + moe_s[...] + ple_term, g_ref[...], b_ref[...])


def _moe(l, x, ple, wp, tm, alpha, triu):
    m = x.shape[0]
    cap = (tm * 3) // 8
    c3 = lambda i: (l, 0, 0)
    c4 = lambda i: (l, 0, 0, 0)
    return pl.pallas_call(
        functools.partial(_moe_kernel, alpha=alpha, cap=cap),
        grid=(m // tm,),
        in_specs=[pl.BlockSpec((tm, D_MODEL), lambda i: (i, 0)),
                  pl.BlockSpec((None, tm, D_PLE), lambda i: (l, i, 0)),
                  _const_spec((None, LANES, D_MODEL), c3), _const_spec((None, LANES, D_MODEL), c3),
                  _const_spec((None, LANES, 1), c3),
                  _const_spec((tm, tm), lambda i: (0, 0)),
                  _const_spec((None, N_EXPERTS, D_MODEL, D_EXPERT), c4),
                  _const_spec((None, N_EXPERTS, D_MODEL, D_EXPERT), c4),
                  _const_spec((None, N_EXPERTS, D_EXPERT, D_MODEL), c4),
                  _const_spec((None, D_MODEL, D_MODEL), c3), _const_spec((None, D_PLE, D_MODEL), c3),
                  _const_spec((None, 1, D_MODEL), c3), _const_spec((None, 1, D_MODEL), c3)],
        out_specs=pl.BlockSpec((tm, D_MODEL), lambda i: (i, 0)),
        out_shape=jax.ShapeDtypeStruct((m, D_MODEL), F32),
        scratch_shapes=[pltpu.VMEM((tm, D_MODEL), F32)],
        compiler_params=_cparams(1), name="moe",
    )(x, ple, wp["wr_hi"], wp["wr_lo"], wp["br"], triu, wp["weg"], wp["weu"], wp["wed"],
      wp["wpg"], wp["wpp"], wp["ln2g"], wp["ln2b"])


def _kron_tril(w_spatial, n_blocks, n):
    w = jnp.tril(w_spatial)[:, :, :n, :n]
    eye = jnp.eye(n_blocks, dtype=w.dtype)
    out = jnp.einsum("ab,lgij->lgaibj", eye, w)
    d = w_spatial.shape[0]
    return out.reshape(d, GB, n_blocks * n, n_blocks * n).astype(BF16)


def _prepare(w_in, b_forget, ln_gmlp_g, ln_gmlp_b, w_spatial, b_spatial, w_conv_c, b_conv_c, ln_conv_g,
             ln_conv_b, w_conv_d, w_branch, w_out, ln1_g, ln1_b, w_router_group, b_router_group,
             w_router_expert, b_router_expert, w_exp_gate, w_exp_up, w_exp_down, w_ple_gate, w_ple_proj,
             ln2_g, ln2_b, tm_p, nb, t, pc, pps):
    d = w_in.shape[0]
    o_f = 3 * W_BR
    o_rest = o_f + HA
    o_gate = o_rest + 7 * W_BR
    r3 = lambda a: a.reshape(d, 1, -1)
    wp = {}
    wp["wmix"] = jnp.concatenate(
        [w_in[:, :, 0:o_f], w_in[:, :, o_rest:o_gate],
         jnp.pad(w_in[:, :, o_f:o_rest], ((0, 0), (0, 0), (0, LANES - HA)))], axis=-1).astype(BF16)
    wp["wgate"] = w_in[:, :, o_gate:].astype(BF16)
    wp["bf"] = jnp.pad(b_forget, ((0, 0), (0, LANES - HA))).reshape(d, 1, LANES)
    gi = np.arange(W_BR) // DB
    wp["gavg"] = jnp.asarray((gi[:, None] == gi[None, :]).astype(np.float32) / DB, dtype=BF16)
    wp["lng"] = r3(ln_gmlp_g)
    wp["lnb"] = r3(ln_gmlp_b)
    wp["msp_p"] = _kron_tril(w_spatial, tm_p // CHUNK, CHUNK)
    wp["msp_s"] = _kron_tril(w_spatial, nb, t)
    bsp = jnp.repeat(jnp.swapaxes(b_spatial, 1, 2), DB, axis=2)
    wp["bsp_p"] = jnp.tile(bsp, (1, tm_p // CHUNK, 1))
    wp["bsp_s"] = jnp.tile(bsp[:, :t, :], (1, nb, 1))
    wp["wc"] = jnp.pad(w_conv_c, ((0, 0), (0, HALO_C - KC), (0, 0)))
    wp["bc"] = r3(b_conv_c)
    wp["lcg"] = r3(ln_conv_g)
    wp["lcb"] = r3(ln_conv_b)
    wp["wd"] = jnp.pad(w_conv_d, ((0, 0), (0, HALO_D - KD), (0, 0)))
    wp["tril_p"] = jnp.asarray(np.tril(np.ones((tm_p, tm_p), np.float32)), dtype=BF16)
    wp["wbr"] = w_branch.astype(BF16)
    wp["wout"] = w_out.astype(BF16)
    wp["ln1g"] = r3(ln1_g)
    wp["ln1b"] = r3(ln1_b)
    wr = jnp.swapaxes(jnp.concatenate([w_router_group, w_router_expert,
                                       jnp.zeros((d, D_MODEL, LANES - N_GROUPS - N_EXPERTS), F32)], axis=-1), 1, 2)
    wr_hi = wr.astype(BF16)
    wp["wr_hi"] = wr_hi
    wp["wr_lo"] = (wr - wr_hi.astype(F32)).astype(BF16)
    wp["br"] = jnp.concatenate([b_router_group, b_router_expert,
                                jnp.zeros((d, LANES - N_GROUPS - N_EXPERTS), F32)], axis=-1).reshape(d, LANES, 1)
    strict_upper = lambda n: jnp.asarray(np.triu(np.ones((n, n), np.float32), 1), dtype=BF16)
    wp["triu_p"] = strict_upper(tm_p)
    wp["triu_s"] = strict_upper(nb * t)
    wp["weg"] = w_exp_gate.astype(BF16)
    wp["weu"] = w_exp_up.astype(BF16)
    wp["wed"] = w_exp_down.astype(BF16)
    wp["wpg"] = w_ple_gate.astype(BF16)
    wp["wpp"] = w_ple_proj.astype(BF16)
    wp["ln2g"] = r3(ln2_g)
    wp["ln2b"] = r3(ln2_b)

    rows = pc * SUBLANES
    ri = np.arange(rows)
    pm = ((ri[:, None] % SUBLANES) == (ri[None, :] % SUBLANES)) & ((ri[None, :] // SUBLANES) < (ri[:, None] // SUBLANES))
    wp["pmat"] = jnp.asarray(pm.astype(np.float32), dtype=BF16)
    a8 = (np.arange(LANES)[None, :] == (np.arange(SUBLANES)[:, None] % HA)).astype(np.float32)
    wp["a8"] = jnp.asarray(a8, dtype=BF16)
    wp["ltri"] = jnp.asarray(np.tril(np.ones((LANES, LANES), np.float32)), dtype=BF16)
    rsel = ((np.arange(HA * t)[None, :] // HA) == np.arange(t)[:, None]).astype(np.float32)
    wp["rsel"] = jnp.asarray(rsel, dtype=BF16)
    wp["upper"] = jnp.asarray(np.triu(np.ones((PAGE, PAGE), np.float32)), dtype=BF16)
    return wp


def kernel(x_prompt, x_sample, cache_k, cache_v, cache_logf, state_conv_c, state_conv_d, page_table,
           p_prompt, p_sample, w_in, b_forget, ln_gmlp_g, ln_gmlp_b, w_spatial, b_spatial, w_conv_c,
           b_conv_c, ln_conv_g, ln_conv_b, w_conv_d, w_branch, w_out, ln1_g, ln1_b, w_router_group,
           b_router_group, w_router_expert, b_router_expert, w_exp_gate, w_exp_up, w_exp_down,
           w_ple_gate, w_ple_proj, ln2_g, ln2_b):
    depth = w_in.shape[0]
    alpha = float((2 * depth) ** 0.25)
    b, s, _ = x_prompt.shape
    nb, t, _ = x_sample.shape
    n_pool = cache_k.shape[1]
    n_pages = page_table.shape[1]
    tm_p = min(512, s)
    tq, tk = min(512, s), min(1024, s)
    ms = nb * t
    pc = min(32, n_pages)
    n_pg = depth * n_pool
    pps = next(c for c in (256, 128, 64, 32, 16, 8, 4, 2, 1) if n_pg % c == 0)

    wp = _prepare(w_in, b_forget, ln_gmlp_g, ln_gmlp_b, w_spatial, b_spatial, w_conv_c, b_conv_c,
                  ln_conv_g, ln_conv_b, w_conv_d, w_branch, w_out, ln1_g, ln1_b, w_router_group,
                  b_router_group, w_router_expert, b_router_expert, w_exp_gate, w_exp_up, w_exp_down,
                  w_ple_gate, w_ple_proj, ln2_g, ln2_b, tm_p, nb, t, pc, pps)

    ck = jnp.transpose(cache_k, (0, 1, 3, 4, 2)).reshape(depth, n_pool, W_BR, PAGE)
    cv = jnp.transpose(cache_v, (0, 1, 3, 4, 2)).reshape(depth, n_pool, W_BR, PAGE)
    lf_t = jnp.transpose(cache_logf, (0, 1, 3, 2))
    lf_rows = jnp.concatenate([lf_t, lf_t], axis=2).reshape(n_pg * SUBLANES, PAGE)
    cc = _page_cumsum(lf_rows, wp["upper"], pps).reshape(depth, n_pool, SUBLANES, LANES)

    pp = p_prompt.reshape(depth, b * s, D_PLE)
    ps = p_sample.reshape(depth, ms, D_PLE)

    xp = x_prompt
    xs = x_sample.reshape(ms, D_MODEL)
    kp, vp, lfp, ccp, cdp = [], [], [], [], []
    ks_, vs_, lfs, ccs, cds, gvs = [], [], [], [], [], []
    for l in range(depth):
        (q, k32, v32, kb, vb, logf, ct, yb, yc, yd, stc, std) = _k1_prompt(l, xp, wp, tm_p)
        ya = _attn_prompt(q, kb, vb, ct, tq, tk)
        f2 = lambda a: a.reshape(b * s, a.shape[-1])
        x1 = _merge(l, f2(xp), f2(ya), f2(yb), f2(yc), f2(yd), wp, tm_p, alpha)
        xp = _moe(l, x1, pp, wp, tm_p, alpha, wp["triu_p"]).reshape(b, s, D_MODEL)
        kp.append(k32.reshape(b, s, HA, DH))
        vp.append(v32.reshape(b, s, HA, DH))
        lfp.append(logf)
        ccp.append(stc[:, HALO_C - (KC - 1):, :])
        cdp.append(std[:, HALO_D - (KD - 1):, :])

        (qs, k_s, v_s, lf_s, vn_s, yb_s, yc_s, yd_s, nsc, nsd) = _k1_sample(
            l, xs, state_conv_c, state_conv_d, wp, nb, t)
        r3 = lambda a: a.reshape(nb, t, a.shape[-1])
        ya_s = _attn_sample(l, page_table, r3(qs), r3(k_s), r3(v_s), r3(lf_s), wp, ck, cv, cc, pc)
        x1s = _merge(l, xs, ya_s.reshape(ms, W_BR), yb_s, yc_s, yd_s, wp, ms, alpha)
        xs = _moe(l, x1s, ps, wp, ms, alpha, wp["triu_s"])
        ks_.append(k_s.reshape(nb, t, HA, DH))
        vs_.append(v_s.reshape(nb, t, HA, DH))
        lfs.append(lf_s[:, 0:HA].reshape(nb, t, HA))
        ccs.append(nsc)
        cds.append(nsd)
        gvs.append(vn_s.reshape(nb, t, GB, DB))

    return (xp, xs.reshape(nb, t, D_MODEL), jnp.stack(kp), jnp.stack(vp), jnp.stack(lfp),
            jnp.stack(ks_), jnp.stack(vs_), jnp.stack(lfs), jnp.stack(ccp), jnp.stack(ccs),
            jnp.stack(cdp), jnp.stack(cds), jnp.stack(gvs))
```

```python
import functools

import numpy as np
import jax
import jax.numpy as jnp
from jax import lax
from jax.experimental import pallas as pl
from jax.experimental.pallas import tpu as pltpu

F32 = jnp.float32
BF16 = jnp.bfloat16

D_MODEL = 1024
W_BR = 256
HA = 4
DH = 64
GB = 4
DB = 64
CHUNK = 128
KC = 31
KD = 3
N_BRANCH = 4
N_GROUPS = 4
EXP_PER_GROUP = 4
N_EXPERTS = 16
D_EXPERT = 256
D_PLE = 256
PAGE = 128
LN_EPS = 1e-5
ATTN_SCALE = DH ** -0.5
LOG2E = 1.4426950408889634
Q_SCALE = ATTN_SCALE * LOG2E
LANES = 128
SUBLANES = 8
HALO_C = 32
HALO_D = 8
W_MIX = 10 * W_BR + LANES
VMEM_LIMIT = 56 * 1024 * 1024


def _cparams(n_axes):
    return pltpu.CompilerParams(dimension_semantics=("arbitrary",) * n_axes,
                                vmem_limit_bytes=VMEM_LIMIT)


def _const_spec(shape, index_map):
    return pl.BlockSpec(shape, index_map, pipeline_mode=pl.Buffered(1))


def _dot(a, b):
    return jnp.dot(a, b, preferred_element_type=F32)


def _dot_nt(a, b):
    return lax.dot_general(a, b, (((1,), (1,)), ((), ())), preferred_element_type=F32)


def _split3(x):
    hi = x.astype(BF16)
    r = x - hi.astype(F32)
    mid = r.astype(BF16)
    lo = (r - mid.astype(F32)).astype(BF16)
    return hi, mid, lo


def _dot3_r(x, m):
    hi, mid, lo = _split3(x)
    return _dot(hi, m) + _dot(mid, m) + _dot(lo, m)


def _dot3_l(m, x):
    hi, mid, lo = _split3(x)
    return _dot(m, hi) + _dot(m, mid) + _dot(m, lo)


def _dot3_nt(m, x):
    hi, mid, lo = _split3(x)
    return _dot_nt(m, hi) + _dot_nt(m, mid) + _dot_nt(m, lo)


def _sigmoid(x):
    return 1.0 / (1.0 + jnp.exp(-x))


def _silu(x):
    return x * _sigmoid(x)


def _gelu(x):
    return 0.5 * x * (1.0 + jnp.tanh(0.7978845608028654 * (x + 0.044715 * (x * x * x))))


def _log_sigmoid(x):
    return jnp.minimum(x, 0.0) - jnp.log1p(jnp.exp(-jnp.abs(x)))


def _layer_norm(x, g, b):
    mu = jnp.mean(x, axis=-1, keepdims=True)
    d = x - mu
    var = jnp.mean(d * d, axis=-1, keepdims=True)
    return d * lax.rsqrt(var + LN_EPS) * g + b


def _lane_group_mask(rows, width, group, g):
    lane = lax.broadcasted_iota(jnp.int32, (rows, width), 1)
    return (lane >= g * group) & (lane < (g + 1) * group)


def _mixer_front(x, wmix_ref, bf_ref):
    z = _dot(x.astype(BF16), wmix_ref[...])
    seg = lambda i: z[:, i * W_BR:(i + 1) * W_BR]
    f = z[:, 10 * W_BR:10 * W_BR + LANES]
    lane = lax.broadcasted_iota(jnp.int32, f.shape, 1)
    logf = jnp.where(lane < HA, _log_sigmoid(f + bf_ref[...]), 0.0)
    return [seg(i) for i in range(10)], logf


def _gmlp(ub, vb, gavg_ref, lng_ref, lnb_ref, msp_ref, bsp_ref):
    rows = ub.shape[0]
    ug = _gelu(ub)
    vg = _gelu(vb)
    gavg = gavg_ref[...]
    mu = _dot3_r(vg, gavg)
    dv = vg - mu
    var = _dot3_r(dv * dv, gavg)
    vn = dv * lax.rsqrt(var + LN_EPS) * lng_ref[...] + lnb_ref[...]
    vnb = vn.astype(BF16)
    mix = bsp_ref[...]
    for g in range(GB):
        mg = _dot(msp_ref[g], vnb)
        mix = mix + jnp.where(_lane_group_mask(rows, W_BR, DB, g), mg, 0.0)
    return ug * mix, vn


def _k1_prompt_kernel(x_ref, wmix_ref, bf_ref, gavg_ref, lng_ref, lnb_ref, msp_ref, bsp_ref,
                      wc_ref, bc_ref, lcg_ref, lcb_ref, wd_ref, tril_ref,
                      q_ref, k_ref, v_ref, kb_ref, vb_ref, logf_ref, ct_ref,
                      yb_ref, yc_ref, yd_ref, sc_ref, sd_ref,
                      xpc, xpd, carry, xsh, *, tm):
    i = pl.program_id(1)

    @pl.when(i == 0)
    def _():
        xpc[0:HALO_C, :] = jnp.zeros((HALO_C, W_BR), F32)
        xpd[0:HALO_D, :] = jnp.zeros((HALO_D, W_BR), F32)
        carry[...] = jnp.zeros_like(carry)

    (q, k, v, ub, vb, ac, gc, bd, cd, hd), logf = _mixer_front(x_ref[0], wmix_ref, bf_ref)
    q_ref[0] = (q * Q_SCALE).astype(BF16)
    k_ref[0] = k
    v_ref[0] = v
    kb_ref[0] = k.astype(BF16)
    vb_ref[0] = v.astype(BF16)
    logf_ref[0] = logf[:, 0:HA]

    c = _dot3_l(tril_ref[...], logf) + carry[0:1, :]
    carry[0:1, :] = c[tm - 1:tm, :]
    ct_ref[0] = c.T[0:SUBLANES, :]

    yb, _ = _gmlp(ub, vb, gavg_ref, lng_ref, lnb_ref, msp_ref, bsp_ref)
    yb_ref[0] = yb.astype(BF16)

    cin = ac * _sigmoid(gc)
    xpc[HALO_C:HALO_C + tm, :] = cin
    span = tm + HALO_C - SUBLANES
    for r in range(1, SUBLANES):
        xsh[r - 1, 0:span, :] = xpc[r:r + span, :]
    acc = jnp.zeros((tm, W_BR), F32) + bc_ref[...]
    for j in range(KC):
        off = HALO_C - (KC - 1) + j
        r = off % SUBLANES
        a = off - r
        win = xpc[a:a + tm, :] if r == 0 else xsh[r - 1, a:a + tm, :]
        acc = acc + wc_ref[j:j + 1, :] * win
    xpc[0:HALO_C, :] = cin[tm - HALO_C:tm, :]
    sc_ref[0] = cin[tm - HALO_C:tm, :]
    yc_ref[0] = _silu(_layer_norm(acc, lcg_ref[...], lcb_ref[...])).astype(BF16)

    din = cd * hd
    xpd[HALO_D:HALO_D + tm, :] = din
    accd = jnp.zeros((tm, W_BR), F32)
    for j in range(KD):
        off = HALO_D - (KD - 1) + j
        accd = accd + wd_ref[j:j + 1, :] * xpd[off:off + tm, :]
    xpd[0:HALO_D, :] = din[tm - HALO_D:tm, :]
    sd_ref[0] = din[tm - HALO_D:tm, :]
    yd_ref[0] = (bd * accd).astype(BF16)


def _k1_prompt(l, x, wp, tm):
    b, s, _ = x.shape
    nt = s // tm
    row = lambda bb, i: (bb, i, 0)
    cst2 = lambda bb, i: (l, 0, 0)
    o256 = lambda dt: jax.ShapeDtypeStruct((b, s, W_BR), dt)
    in_specs = [
        pl.BlockSpec((1, tm, D_MODEL), row),
        _const_spec((None, D_MODEL, W_MIX), cst2),
        _const_spec((None, 1, LANES), cst2),
        _const_spec((W_BR, W_BR), lambda bb, i: (0, 0)),
        _const_spec((None, 1, W_BR), cst2),
        _const_spec((None, 1, W_BR), cst2),
        _const_spec((None, GB, tm, tm), lambda bb, i: (l, 0, 0, 0)),
        _const_spec((None, tm, W_BR), cst2),
        _const_spec((None, HALO_C, W_BR), cst2),
        _const_spec((None, 1, W_BR), cst2),
        _const_spec((None, 1, W_BR), cst2),
        _const_spec((None, 1, W_BR), cst2),
        _const_spec((None, HALO_D, W_BR), cst2),
        _const_spec((tm, tm), lambda bb, i: (0, 0)),
    ]
    out_shape = [o256(BF16), o256(F32), o256(F32), o256(BF16), o256(BF16),
                 jax.ShapeDtypeStruct((b, s, HA), F32),
                 jax.ShapeDtypeStruct((b, SUBLANES, s), F32),
                 o256(BF16), o256(BF16), o256(BF16),
                 jax.ShapeDtypeStruct((b, HALO_C, W_BR), F32),
                 jax.ShapeDtypeStruct((b, HALO_D, W_BR), F32)]
    blk = pl.BlockSpec((1, tm, W_BR), row)
    out_specs = [blk, blk, blk, blk, blk,
                 pl.BlockSpec((1, tm, HA), row),
                 pl.BlockSpec((1, SUBLANES, tm), lambda bb, i: (bb, 0, i)),
                 blk, blk, blk,
                 pl.BlockSpec((1, HALO_C, W_BR), lambda bb, i: (bb, 0, 0)),
                 pl.BlockSpec((1, HALO_D, W_BR), lambda bb, i: (bb, 0, 0))]
    return pl.pallas_call(
        functools.partial(_k1_prompt_kernel, tm=tm),
        grid=(b, nt), in_specs=in_specs, out_specs=out_specs, out_shape=out_shape,
        scratch_shapes=[pltpu.VMEM((tm + HALO_C, W_BR), F32), pltpu.VMEM((tm + HALO_D, W_BR), F32),
                        pltpu.VMEM((SUBLANES, LANES), F32),
                        pltpu.VMEM((SUBLANES - 1, tm + HALO_C - SUBLANES, W_BR), F32)],
        compiler_params=_cparams(2), name="k1_prompt",
    )(x, wp["wmix"], wp["bf"], wp["gavg"], wp["lng"], wp["lnb"], wp["msp_p"], wp["bsp_p"],
      wp["wc"], wp["bc"], wp["lcg"], wp["lcb"], wp["wd"], wp["tril_p"])


def _k1_sample_kernel(x_ref, wmix_ref, bf_ref, gavg_ref, lng_ref, lnb_ref, msp_ref, bsp_ref,
                      wc_ref, bc_ref, lcg_ref, lcb_ref, wd_ref, stc_ref, std_ref,
                      q_ref, k_ref, v_ref, logf_ref, vn_ref, yb_ref, yc_ref, yd_ref, nsc_ref, nsd_ref,
                      cin_s, ypre_s, din_s, ydpre_s, xs, xsd, *, nb, t):
    (q, k, v, ub, vb, ac, gc, bd, cd, hd), logf = _mixer_front(x_ref[...], wmix_ref, bf_ref)
    q_ref[...] = q * Q_SCALE
    k_ref[...] = k
    v_ref[...] = v
    logf_ref[...] = logf
    yb, vn = _gmlp(ub, vb, gavg_ref, lng_ref, lnb_ref, msp_ref, bsp_ref)
    yb_ref[...] = yb.astype(BF16)
    vn_ref[...] = vn

    cin_s[...] = ac * _sigmoid(gc)
    din_s[...] = cd * hd

    def body(bb, _):
        r0 = pl.multiple_of(bb * t, t)
        xs[0:KC - 1, :] = stc_ref[bb]
        xs[KC - 1:KC - 1 + t, :] = cin_s[pl.ds(r0, t), :]
        acc = jnp.zeros((t, W_BR), F32) + bc_ref[...]
        for j in range(KC):
            acc = acc + wc_ref[j:j + 1, :] * xs[j:j + t, :]
        ypre_s[pl.ds(r0, t), :] = acc
        nsc_ref[bb] = xs[t:t + KC - 1, :]

        xsd[0:KD - 1, :] = std_ref[bb]
        xsd[KD - 1:KD - 1 + t, :] = din_s[pl.ds(r0, t), :]
        accd = jnp.zeros((t, W_BR), F32)
        for j in range(KD):
            accd = accd + wd_ref[j:j + 1, :] * xsd[j:j + t, :]
        ydpre_s[pl.ds(r0, t), :] = accd
        nsd_ref[bb] = xsd[t:t + KD - 1, :]
        return 0

    lax.fori_loop(0, nb, body, 0)
    yc_ref[...] = _silu(_layer_norm(ypre_s[...], lcg_ref[...], lcb_ref[...])).astype(BF16)
    yd_ref[...] = (bd * ydpre_s[...]).astype(BF16)


def _k1_sample(l, x, stc, std, wp, nb, t):
    m = nb * t
    cst2 = lambda i: (l, 0, 0)
    cst4 = lambda i: (l, 0, 0, 0)
    full = lambda shp: pl.BlockSpec(shp, lambda i: (0,) * len(shp))
    in_specs = [
        full((m, D_MODEL)),
        pl.BlockSpec((None, D_MODEL, W_MIX), cst2),
        pl.BlockSpec((None, 1, LANES), cst2),
        full((W_BR, W_BR)),
        pl.BlockSpec((None, 1, W_BR), cst2),
        pl.BlockSpec((None, 1, W_BR), cst2),
        pl.BlockSpec((None, GB, m, m), cst4),
        pl.BlockSpec((None, m, W_BR), cst2),
        pl.BlockSpec((None, HALO_C, W_BR), cst2),
        pl.BlockSpec((None, 1, W_BR), cst2),
        pl.BlockSpec((None, 1, W_BR), cst2),
        pl.BlockSpec((None, 1, W_BR), cst2),
        pl.BlockSpec((None, HALO_D, W_BR), cst2),
        pl.BlockSpec((None, nb, KC - 1, W_BR), cst4),
        pl.BlockSpec((None, nb, KD - 1, W_BR), cst4),
    ]
    o = lambda dt: jax.ShapeDtypeStruct((m, W_BR), dt)
    out_shape = [o(F32), o(F32), o(F32), jax.ShapeDtypeStruct((m, LANES), F32), o(F32),
                 o(BF16), o(BF16), o(BF16),
                 jax.ShapeDtypeStruct((nb, KC - 1, W_BR), F32),
                 jax.ShapeDtypeStruct((nb, KD - 1, W_BR), F32)]
    out_specs = [full(s.shape) for s in out_shape]
    return pl.pallas_call(
        functools.partial(_k1_sample_kernel, nb=nb, t=t),
        grid=(1,), in_specs=in_specs, out_specs=out_specs, out_shape=out_shape,
        scratch_shapes=[pltpu.VMEM((m, W_BR), F32), pltpu.VMEM((m, W_BR), F32),
                        pltpu.VMEM((m, W_BR), F32), pltpu.VMEM((m, W_BR), F32),
                        pltpu.VMEM((KC - 1 + t + 2, W_BR), F32), pltpu.VMEM((2 * SUBLANES, W_BR), F32)],
        compiler_params=pltpu.CompilerParams(vmem_limit_bytes=VMEM_LIMIT), name="k1_sample",
    )(x, wp["wmix"], wp["bf"], wp["gavg"], wp["lng"], wp["lnb"], wp["msp_s"], wp["bsp_s"],
      wp["wc"], wp["bc"], wp["lcg"], wp["lcb"], wp["wd"], stc, std)


def _attn_prompt_kernel(qi_tab, ki_tab, q_ref, k_ref, v_ref, ck_ref, cq_ref, o_ref,
                        qm_s, m_s, acc_s, *, tq, tk):
    p = pl.program_id(1)
    qi = qi_tab[p]
    ki = ki_tab[p]
    last_k = ((qi + 1) * tq - 1) // tk

    @pl.when(ki == 0)
    def _():
        q = q_ref[0]
        for h in range(HA):
            qm_s[h] = jnp.where(_lane_group_mask(tq, W_BR, DH, h), q, jnp.zeros_like(q))
        m_s[...] = jnp.full(m_s.shape, -jnp.inf, F32)
        acc_s[...] = jnp.zeros(acc_s.shape, F32)

    def step(masked, nk):
        k = k_ref[0, 0:nk, :]
        v = v_ref[0, 0:nk, :]
        ck = ck_ref[0, :, 0:nk]
        cq = cq_ref[0]
        if masked:
            keep = ((ki * tk - qi * tq) + lax.broadcasted_iota(jnp.int32, (tq, nk), 1)
                    <= lax.broadcasted_iota(jnp.int32, (tq, nk), 0))
        for h in range(HA):
            bias = (cq[h:h + 1, 0:1] - ck[h:h + 1, :]) * LOG2E
            s = _dot_nt(qm_s[h], k) + bias
            if masked:
                s = jnp.where(keep, s, -jnp.inf)
            m_prev = m_s[h]
            m_new = jnp.maximum(m_prev, jnp.max(s, axis=1, keepdims=True))
            alpha = jnp.exp2(m_prev - m_new)
            pr = jnp.exp2(s - m_new).astype(BF16)
            v_h = jnp.where(_lane_group_mask(nk, W_BR, DH, h), v, jnp.ones_like(v))
            acc_s[h] = acc_s[h] * alpha + _dot(pr, v_h)
            m_s[h] = m_new

    n_visible = (qi + 1) * tq - ki * tk
    half = tk // 2

    @pl.when(n_visible >= tk + tq)
    def _():
        step(False, tk)

    @pl.when((n_visible < tk + tq) & (n_visible > half))
    def _():
        step(True, tk)

    @pl.when(n_visible <= half)
    def _():
        step(True, half)

    @pl.when(ki == last_k)
    def _():
        out = jnp.zeros((tq, W_BR), F32)
        for h in range(HA):
            acc = acc_s[h]
            den_lane = ((h + 1) % HA) * DH
            out = out + jnp.where(_lane_group_mask(tq, W_BR, DH, h), acc / acc[:, den_lane:den_lane + 1], 0.0)
        o_ref[0] = out.astype(BF16)


def _attn_prompt(q, k, v, ct, tq, tk):
    b, s, _ = q.shape
    nq = s // tq
    n_k = [((i + 1) * tq - 1) // tk + 1 for i in range(nq)]
    qi_np = np.concatenate([np.full(n_k[i], i, np.int32) for i in range(nq)])
    ki_np = np.concatenate([np.arange(n_k[i], dtype=np.int32) for i in range(nq)])
    n_pairs = int(qi_np.shape[0])
    grid_spec = pltpu.PrefetchScalarGridSpec(
        num_scalar_prefetch=2, grid=(b, n_pairs),
        in_specs=[
            pl.BlockSpec((1, tq, W_BR), lambda bb, p, qt, kt: (bb, qt[p], 0)),
            pl.BlockSpec((1, tk, W_BR), lambda bb, p, qt, kt: (bb, kt[p], 0)),
            pl.BlockSpec((1, tk, W_BR), lambda bb, p, qt, kt: (bb, kt[p], 0)),
            pl.BlockSpec((1, SUBLANES, tk), lambda bb, p, qt, kt: (bb, 0, kt[p])),
            pl.BlockSpec((1, SUBLANES, tq), lambda bb, p, qt, kt: (bb, 0, qt[p])),
        ],
        out_specs=pl.BlockSpec((1, tq, W_BR), lambda bb, p, qt, kt: (bb, qt[p], 0)),
        scratch_shapes=[pltpu.VMEM((HA, tq, W_BR), BF16), pltpu.VMEM((HA, tq, 1), F32),
                        pltpu.VMEM((HA, tq, W_BR), F32)],
    )
    return pl.pallas_call(
        functools.partial(_attn_prompt_kernel, tq=tq, tk=tk),
        grid_spec=grid_spec, out_shape=jax.ShapeDtypeStruct((b, s, W_BR), BF16),
        compiler_params=_cparams(2), name="attn_prompt",
    )(jnp.asarray(qi_np), jnp.asarray(ki_np), q, k, v, ct, ct)


def _page_cumsum_kernel(lf_ref, u_ref, o_ref):
    o_ref[...] = _dot3_r(lf_ref[...], u_ref[...])


def _page_cumsum(lf_rows, upper, pps):
    n_rows = lf_rows.shape[0]
    return pl.pallas_call(
        _page_cumsum_kernel, grid=(n_rows // (pps * SUBLANES),),
        in_specs=[pl.BlockSpec((pps * SUBLANES, PAGE), lambda i: (i, 0)),
                  _const_spec((PAGE, PAGE), lambda i: (0, 0))],
        out_specs=pl.BlockSpec((pps * SUBLANES, PAGE), lambda i: (i, 0)),
        out_shape=jax.ShapeDtypeStruct((n_rows, PAGE), F32),
        compiler_params=_cparams(1), name="page_cumsum",
    )(lf_rows, upper)


def _attn_sample_kernel(pt_ref, q_ref, kn_ref, vn_ref, lfn_ref, pmat_ref, a8_ref, ltri_ref, rsel_ref,
                        ck_hbm, cv_hbm, cc_hbm, o_ref, kbuf, vbuf, cbuf, sem, m_s, l_s, acc_s, carry_s,
                        *, l, nb, t, n_pages, pc):
    b = pl.program_id(0)
    c = pl.program_id(1)
    nc = n_pages // pc
    step_id = b * nc + c
    slot = step_id % 2
    rows = HA * t
    ng = rows // SUBLANES

    def copies(bb, cc, sl, p):
        pg = pt_ref[bb, cc * pc + p]
        return (pltpu.make_async_copy(ck_hbm.at[l, pg], kbuf.at[sl, p], sem.at[sl, 0]),
                pltpu.make_async_copy(cv_hbm.at[l, pg], vbuf.at[sl, p], sem.at[sl, 1]),
                pltpu.make_async_copy(cc_hbm.at[l, pg], cbuf.at[sl, p], sem.at[sl, 2]))

    def issue(bb, cc, sl):
        def body(p, _):
            for cp in copies(bb, cc, sl, p):
                cp.start()
            return 0
        lax.fori_loop(0, pc, body, 0)

    @pl.when(step_id == 0)
    def _():
        issue(0, 0, 0)

    @pl.when(step_id + 1 < nb * nc)
    def _():
        nxt = step_id + 1
        issue(nxt // nc, nxt % nc, 1 - slot)

    def wait_body(p, _):
        for cp in copies(b, c, slot, p):
            cp.wait()
        return 0
    lax.fori_loop(0, pc, wait_body, 0)

    @pl.when(c == 0)
    def _():
        m_s[...] = jnp.full(m_s.shape, -jnp.inf, F32)
        l_s[...] = jnp.zeros(l_s.shape, F32)
        acc_s[...] = jnp.zeros(acc_s.shape, F32)
        carry_s[...] = jnp.zeros(carry_s.shape, F32)

    q = q_ref[0]
    sub = lax.broadcasted_iota(jnp.int32, (SUBLANES, W_BR), 0)
    lane = lax.broadcasted_iota(jnp.int32, (SUBLANES, W_BR), 1)
    head_ok = (lane // DH) == (sub % HA)
    qm = []
    for g in range(t // 2):
        qa = jnp.broadcast_to(q[2 * g:2 * g + 1, :], (SUBLANES, W_BR))
        qb = jnp.broadcast_to(q[2 * g + 1:2 * g + 2, :], (SUBLANES, W_BR))
        qm.append(jnp.where(head_ok, jnp.where(sub < HA, qa, qb), 0.0))
    qm = jnp.concatenate(qm, axis=0).astype(BF16)

    def online_update(s3, pv_fn):
        m_prev = m_s[...]
        m_new = jnp.maximum(m_prev, jnp.max(s3, axis=2, keepdims=True))
        alpha = jnp.exp2(m_prev - m_new)
        pr = jnp.exp2(s3 - m_new)
        l_s[...] = alpha * l_s[...] + jnp.sum(pr, axis=2, keepdims=True)
        acc_s[...] = acc_s[...] * alpha.reshape(rows, 1) + pv_fn(pr.reshape(rows, s3.shape[2]).astype(BF16))
        m_s[...] = m_new

    c2 = cbuf[slot].reshape(pc * SUBLANES, LANES)
    off = _dot3_l(pmat_ref[...], c2)
    carry = jnp.concatenate([carry_s[:, 0:1]] * pc, axis=0)
    cf = c2 + off[:, LANES - 1:LANES] + carry
    tot = cf[(pc - 1) * SUBLANES:pc * SUBLANES, LANES - 1:LANES]
    carry_s[...] = jnp.broadcast_to(tot, carry_s.shape)
    c_past = jnp.concatenate([cf[j * SUBLANES:(j + 1) * SUBLANES, :] for j in range(pc)], axis=1)

    s_past = jnp.concatenate([_dot(qm, kbuf[slot, j].astype(BF16)) for j in range(pc)], axis=1)
    s_past = s_past.reshape(ng, SUBLANES, pc * PAGE) - (c_past * LOG2E)[None]

    def pv_past(p2):
        acc = jnp.zeros((rows, W_BR), F32)
        for j in range(pc):
            acc = acc + _dot_nt(p2[:, j * PAGE:(j + 1) * PAGE], vbuf[slot, j].astype(BF16))
        return acc

    online_update(s_past, pv_past)

    @pl.when(c == nc - 1)
    def _():
        lfp = jnp.concatenate([lfn_ref[0], jnp.zeros((LANES - t, LANES), F32)], axis=0)
        cum_new = _dot3_l(ltri_ref[...], lfp)
        c_new = _dot3_nt(a8_ref[...], cum_new) + tot
        zpad = jnp.zeros((LANES - t, W_BR), F32)
        kn = jnp.concatenate([kn_ref[0], zpad], axis=0).astype(BF16)
        vn = jnp.concatenate([vn_ref[0], zpad], axis=0).astype(BF16)
        s_new = _dot_nt(qm, kn).reshape(ng, SUBLANES, LANES) - (c_new * LOG2E)[None]
        gi = lax.broadcasted_iota(jnp.int32, s_new.shape, 0)
        si = lax.broadcasted_iota(jnp.int32, s_new.shape, 1)
        ji = lax.broadcasted_iota(jnp.int32, s_new.shape, 2)
        s_new = jnp.where(ji <= 2 * gi + si // HA, s_new, -jnp.inf)
        online_update(s_new, lambda p2: _dot(p2, vn))

        o = acc_s[...] / l_s[...].reshape(rows, 1)
        sub_r = lax.broadcasted_iota(jnp.int32, (rows, W_BR), 0)
        lane_r = lax.broadcasted_iota(jnp.int32, (rows, W_BR), 1)
        om = jnp.where((lane_r // DH) == (sub_r % HA), o, 0.0)
        o_ref[0] = _dot(rsel_ref[...], om.astype(BF16))


def _attn_sample(l, page_table, q, kn, vn, lfn, wp, ck, cv, cc, pc):
    nb, t, _ = q.shape
    n_pages = page_table.shape[1]
    rows = HA * t
    cst = lambda shp: pl.BlockSpec(shp, lambda b, c, pt: (0,) * len(shp))
    per_b = lambda w: pl.BlockSpec((1, t, w), lambda b, c, pt: (b, 0, 0))
    grid_spec = pltpu.PrefetchScalarGridSpec(
        num_scalar_prefetch=1, grid=(nb, n_pages // pc),
        in_specs=[per_b(W_BR), per_b(W_BR), per_b(W_BR), per_b(LANES),
                  cst((pc * SUBLANES, pc * SUBLANES)), cst((SUBLANES, LANES)),
                  cst((LANES, LANES)), cst((t, rows)),
                  pl.BlockSpec(memory_space=pl.ANY), pl.BlockSpec(memory_space=pl.ANY),
                  pl.BlockSpec(memory_space=pl.ANY)],
        out_specs=per_b(W_BR),
        scratch_shapes=[pltpu.VMEM((2, pc, W_BR, PAGE), F32),
                        pltpu.VMEM((2, pc, W_BR, PAGE), F32),
                        pltpu.VMEM((2, pc, SUBLANES, LANES), F32),
                        pltpu.SemaphoreType.DMA((2, 3)),
                        pltpu.VMEM((rows // SUBLANES, SUBLANES, 1), F32),
                        pltpu.VMEM((rows // SUBLANES, SUBLANES, 1), F32),
                        pltpu.VMEM((rows, W_BR), F32),
                        pltpu.VMEM((SUBLANES, LANES), F32)],
    )
    return pl.pallas_call(
        functools.partial(_attn_sample_kernel, l=l, nb=nb, t=t, n_pages=n_pages, pc=pc),
        grid_spec=grid_spec, out_shape=jax.ShapeDtypeStruct((nb, t, W_BR), F32),
        compiler_params=_cparams(2), name="attn_sample",
    )(page_table, q, kn, vn, lfn, wp["pmat"], wp["a8"], wp["ltri"], wp["rsel"], ck, cv, cc)


def _merge_kernel(x_ref, ya_ref, yb_ref, yc_ref, yd_ref, wg_ref, wbr_ref, wo_ref, g_ref, b_ref, o_ref,
                  *, alpha):
    x = x_ref[...]
    xb = x.astype(BF16)
    ys = (ya_ref, yb_ref, yc_ref, yd_ref)
    acc = jnp.zeros(x.shape, F32)
    for n in range(N_BRANCH):
        gate = _sigmoid(_dot(xb, wg_ref[:, n * D_MODEL:(n + 1) * D_MODEL]))
        acc = acc + gate * _dot(ys[n][...].astype(BF16), wbr_ref[n])
    mix = _dot(acc.astype(BF16), wo_ref[...])
    o_ref[...] = _layer_norm(alpha * x + mix, g_ref[...], b_ref[...])


def _merge(l, x, ya, yb, yc, yd, wp, tm, alpha):
    m = x.shape[0]
    row = lambda w: pl.BlockSpec((tm, w), lambda i: (i, 0))
    c3 = lambda i: (l, 0, 0)
    return pl.pallas_call(
        functools.partial(_merge_kernel, alpha=alpha),
        grid=(m // tm,),
        in_specs=[row(D_MODEL), row(W_BR), row(W_BR), row(W_BR), row(W_BR),
                  _const_spec((None, D_MODEL, N_BRANCH * D_MODEL), c3),
                  _const_spec((None, N_BRANCH, W_BR, D_MODEL), lambda i: (l, 0, 0, 0)),
                  _const_spec((None, D_MODEL, D_MODEL), c3),
                  _const_spec((None, 1, D_MODEL), c3), _const_spec((None, 1, D_MODEL), c3)],
        out_specs=row(D_MODEL), out_shape=jax.ShapeDtypeStruct((m, D_MODEL), F32),
        compiler_params=_cparams(1), name="merge",
    )(x, ya, yb, yc, yd, wp["wgate"], wp["wbr"], wp["wout"], wp["ln1g"], wp["ln1b"])


def _route_tokens_on_lanes(lt):
    sub = lax.broadcasted_iota(jnp.int32, lt.shape, 0)
    big = jnp.int32(LANES)
    lg = jnp.where(sub < N_GROUPS, lt, -jnp.inf)
    mg = jnp.max(lg, axis=0, keepdims=True)
    p_top = 1.0 / jnp.sum(jnp.exp(lg - mg), axis=0, keepdims=True)
    g_idx = jnp.min(jnp.where(lg == mg, sub, big), axis=0, keepdims=True)
    lo = N_GROUPS + g_idx * EXP_PER_GROUP
    le = jnp.where((sub >= lo) & (sub < lo + EXP_PER_GROUP), lt, -jnp.inf)
    v1 = jnp.max(le, axis=0, keepdims=True)
    i1 = jnp.min(jnp.where(le == v1, sub, big), axis=0, keepdims=True)
    le2 = jnp.where(sub == i1, -jnp.inf, le)
    v2 = jnp.max(le2, axis=0, keepdims=True)
    i2 = jnp.min(jnp.where(le2 == v2, sub, big), axis=0, keepdims=True)
    e2 = jnp.exp(v2 - v1)
    w1 = p_top / (1.0 + e2)
    w2 = p_top * e2 / (1.0 + e2)
    comb_t = jnp.where(sub == i1, w1, 0.0) + jnp.where(sub == i2, w2, 0.0)
    return g_idx, comb_t


def _moe_kernel(x_ref, ple_ref, wrh_ref, wrl_ref, br_ref, triu_ref, wg_ref, wu_ref, wd_ref, wpg_ref, wpp_ref,
                g_ref, b_ref, o_ref, moe_s, *, alpha, cap):
    x = x_ref[...]
    tm = x.shape[0]
    xb = x.astype(BF16)
    xm = (x - xb.astype(F32)).astype(BF16)
    nr = 3 * SUBLANES
    kpad = -(-cap // LANES) * LANES

    wrh = wrh_ref[...]
    lt = (_dot_nt(wrh, xb) + _dot_nt(wrl_ref[...], xb) + _dot_nt(wrh, xm) + br_ref[...])[0:nr]
    g_idx, comb_t = _route_tokens_on_lanes(lt)

    sub16 = lax.broadcasted_iota(jnp.int32, (2 * SUBLANES, tm), 0)
    onehot = jnp.where(sub16 == g_idx, 1.0, 0.0)
    before = _dot(onehot.astype(BF16), triu_ref[...])
    rank = jnp.sum(onehot * before, axis=0, keepdims=True)
    n_max = jnp.max(before[:, tm - 1:tm] + onehot[:, tm - 1:tm])

    sub8 = lax.broadcasted_iota(jnp.int32, (SUBLANES, tm), 0)
    extra = jnp.where(sub8 == 0, rank, jnp.where(sub8 == 1, g_idx.astype(F32), 0.0))
    col = jnp.concatenate([comb_t, extra, jnp.zeros((LANES - nr - SUBLANES, tm), F32)], axis=0).T
    rank_c = col[:, nr:nr + 1].astype(jnp.int32)
    g_c = col[:, nr + 1:nr + 2].astype(jnp.int32)

    def experts(rows_b, comb_rows, e_list):
        y = jnp.zeros((rows_b.shape[0], D_MODEL), F32)
        for e in e_list:
            hg = _dot(rows_b, wg_ref[e])
            hu = _dot(rows_b, wu_ref[e])
            hh = _silu(hg) * hu * comb_rows[:, N_GROUPS + e:N_GROUPS + e + 1]
            y = y + _dot(hh.astype(BF16), wd_ref[e])
        return y

    @pl.when(n_max <= cap)
    def _():
        comb_hi = col.astype(BF16)
        comb_lo = (col - comb_hi.astype(F32)).astype(BF16)
        r_iota = lax.broadcasted_iota(jnp.int32, (cap, tm), 0)
        c_iota = lax.broadcasted_iota(jnp.int32, (tm, kpad), 1)
        rank_i = rank.astype(jnp.int32)
        acc = jnp.zeros((tm, D_MODEL), F32)
        for g in range(N_GROUPS):
            pg = jnp.where((r_iota == rank_i) & (g_idx == g), 1.0, 0.0).astype(BF16)
            pgt = jnp.where((c_iota == rank_c) & (g_c == g), 1.0, 0.0).astype(BF16)
            xs = _dot(pg, xb).astype(BF16)
            cs = _dot(pg, comb_hi) + _dot(pg, comb_lo)
            y = experts(xs, cs, range(g * EXP_PER_GROUP, (g + 1) * EXP_PER_GROUP))
            if kpad > cap:
                y = jnp.concatenate([y, jnp.zeros((kpad - cap, D_MODEL), F32)], axis=0)
            acc = acc + _dot(pgt, y.astype(BF16))
        moe_s[...] = acc

    @pl.when(n_max > cap)
    def _():
        moe_s[...] = experts(xb, col, range(N_EXPERTS))

    ple_term = _sigmoid(_dot(xb, wpg_ref[...])) * _dot(ple_ref[...].astype(BF16), wpp_ref[...])
    o_ref[...] = _layer_norm(alpha * x + moe_s[...] + ple_term, g_ref[...], b_ref[...])


def _moe(l, x, ple, wp, tm, alpha, triu):
    m = x.shape[0]
    cap = (tm * 5) // 16
    c3 = lambda i: (l, 0, 0)
    c4 = lambda i: (l, 0, 0, 0)
    return pl.pallas_call(
        functools.partial(_moe_kernel, alpha=alpha, cap=cap),
        grid=(m // tm,),
        in_specs=[pl.BlockSpec((tm, D_MODEL), lambda i: (i, 0)),
                  pl.BlockSpec((None, tm, D_PLE), lambda i: (l, i, 0)),
                  _const_spec((None, LANES, D_MODEL), c3), _const_spec((None, LANES, D_MODEL), c3),
                  _const_spec((None, LANES, 1), c3),
                  _const_spec((tm, tm), lambda i: (0, 0)),
                  _const_spec((None, N_EXPERTS, D_MODEL, D_EXPERT), c4),
                  _const_spec((None, N_EXPERTS, D_MODEL, D_EXPERT), c4),
                  _const_spec((None, N_EXPERTS, D_EXPERT, D_MODEL), c4),
                  _const_spec((None, D_MODEL, D_MODEL), c3), _const_spec((None, D_PLE, D_MODEL), c3),
                  _const_spec((None, 1, D_MODEL), c3), _const_spec((None, 1, D_MODEL), c3)],
        out_specs=pl.BlockSpec((tm, D_MODEL), lambda i: (i, 0)),
        out_shape=jax.ShapeDtypeStruct((m, D_MODEL), F32),
        scratch_shapes=[pltpu.VMEM((tm, D_MODEL), F32)],
        compiler_params=_cparams(1), name="moe",
    )(x, ple, wp["wr_hi"], wp["wr_lo"], wp["br"], triu, wp["weg"], wp["weu"], wp["wed"],
      wp["wpg"], wp["wpp"], wp["ln2g"], wp["ln2b"])


def _kron_tril(w_spatial, n_blocks, n):
    w = jnp.tril(w_spatial)[:, :, :n, :n]
    eye = jnp.eye(n_blocks, dtype=w.dtype)
    out = jnp.einsum("ab,lgij->lgaibj", eye, w)
    d = w_spatial.shape[0]
    return out.reshape(d, GB, n_blocks * n, n_blocks * n).astype(BF16)


def _prepare(w_in, b_forget, ln_gmlp_g, ln_gmlp_b, w_spatial, b_spatial, w_conv_c, b_conv_c, ln_conv_g,
             ln_conv_b, w_conv_d, w_branch, w_out, ln1_g, ln1_b, w_router_group, b_router_group,
             w_router_expert, b_router_expert, w_exp_gate, w_exp_up, w_exp_down, w_ple_gate, w_ple_proj,
             ln2_g, ln2_b, tm_p, nb, t, pc, pps):
    d = w_in.shape[0]
    o_f = 3 * W_BR
    o_rest = o_f + HA
    o_gate = o_rest + 7 * W_BR
    r3 = lambda a: a.reshape(d, 1, -1)
    wp = {}
    wp["wmix"] = jnp.concatenate(
        [w_in[:, :, 0:o_f], w_in[:, :, o_rest:o_gate],
         jnp.pad(w_in[:, :, o_f:o_rest], ((0, 0), (0, 0), (0, LANES - HA)))], axis=-1).astype(BF16)
    wp["wgate"] = w_in[:, :, o_gate:].astype(BF16)
    wp["bf"] = jnp.pad(b_forget, ((0, 0), (0, LANES - HA))).reshape(d, 1, LANES)
    gi = np.arange(W_BR) // DB
    wp["gavg"] = jnp.asarray((gi[:, None] == gi[None, :]).astype(np.float32) / DB, dtype=BF16)
    wp["lng"] = r3(ln_gmlp_g)
    wp["lnb"] = r3(ln_gmlp_b)
    wp["msp_p"] = _kron_tril(w_spatial, tm_p // CHUNK, CHUNK)
    wp["msp_s"] = _kron_tril(w_spatial, nb, t)
    bsp = jnp.repeat(jnp.swapaxes(b_spatial, 1, 2), DB, axis=2)
    wp["bsp_p"] = jnp.tile(bsp, (1, tm_p // CHUNK, 1))
    wp["bsp_s"] = jnp.tile(bsp[:, :t, :], (1, nb, 1))
    wp["wc"] = jnp.pad(w_conv_c, ((0, 0), (0, HALO_C - KC), (0, 0)))
    wp["bc"] = r3(b_conv_c)
    wp["lcg"] = r3(ln_conv_g)
    wp["lcb"] = r3(ln_conv_b)
    wp["wd"] = jnp.pad(w_conv_d, ((0, 0), (0, HALO_D - KD), (0, 0)))
    wp["tril_p"] = jnp.asarray(np.tril(np.ones((tm_p, tm_p), np.float32)), dtype=BF16)
    wp["wbr"] = w_branch.astype(BF16)
    wp["wout"] = w_out.astype(BF16)
    wp["ln1g"] = r3(ln1_g)
    wp["ln1b"] = r3(ln1_b)
    wr = jnp.swapaxes(jnp.concatenate([w_router_group, w_router_expert,
                                       jnp.zeros((d, D_MODEL, LANES - N_GROUPS - N_EXPERTS), F32)], axis=-1), 1, 2)
    wr_hi = wr.astype(BF16)
    wp["wr_hi"] = wr_hi
    wp["wr_lo"] = (wr - wr_hi.astype(F32)).astype(BF16)
    wp["br"] = jnp.concatenate([b_router_group, b_router_expert,
                                jnp.zeros((d, LANES - N_GROUPS - N_EXPERTS), F32)], axis=-1).reshape(d, LANES, 1)
    strict_upper = lambda n: jnp.asarray(np.triu(np.ones((n, n), np.float32), 1), dtype=BF16)
    wp["triu_p"] = strict_upper(tm_p)
    wp["triu_s"] = strict_upper(nb * t)
    wp["weg"] = w_exp_gate.astype(BF16)
    wp["weu"] = w_exp_up.astype(BF16)
    wp["wed"] = w_exp_down.astype(BF16)
    wp["wpg"] = w_ple_gate.astype(BF16)
    wp["wpp"] = w_ple_proj.astype(BF16)
    wp["ln2g"] = r3(ln2_g)
    wp["ln2b"] = r3(ln2_b)

    rows = pc * SUBLANES
    ri = np.arange(rows)
    pm = ((ri[:, None] % SUBLANES) == (ri[None, :] % SUBLANES)) & ((ri[None, :] // SUBLANES) < (ri[:, None] // SUBLANES))
    wp["pmat"] = jnp.asarray(pm.astype(np.float32), dtype=BF16)
    a8 = (np.arange(LANES)[None, :] == (np.arange(SUBLANES)[:, None] % HA)).astype(np.float32)
    wp["a8"] = jnp.asarray(a8, dtype=BF16)
    wp["ltri"] = jnp.asarray(np.tril(np.ones((LANES, LANES), np.float32)), dtype=BF16)
    rsel = ((np.arange(HA * t)[None, :] // HA) == np.arange(t)[:, None]).astype(np.float32)
    wp["rsel"] = jnp.asarray(rsel, dtype=BF16)
    wp["upper"] = jnp.asarray(np.triu(np.ones((PAGE, PAGE), np.float32)), dtype=BF16)
    return wp


def kernel(x_prompt, x_sample, cache_k, cache_v, cache_logf, state_conv_c, state_conv_d, page_table,
           p_prompt, p_sample, w_in, b_forget, ln_gmlp_g, ln_gmlp_b, w_spatial, b_spatial, w_conv_c,
           b_conv_c, ln_conv_g, ln_conv_b, w_conv_d, w_branch, w_out, ln1_g, ln1_b, w_router_group,
           b_router_group, w_router_expert, b_router_expert, w_exp_gate, w_exp_up, w_exp_down,
           w_ple_gate, w_ple_proj, ln2_g, ln2_b):
    depth = w_in.shape[0]
    alpha = float((2 * depth) ** 0.25)
    b, s, _ = x_prompt.shape
    nb, t, _ = x_sample.shape
    n_pool = cache_k.shape[1]
    n_pages = page_table.shape[1]
    tm_p = min(512, s)
    tq, tk = min(512, s), min(1024, s)
    ms = nb * t
    pc = min(64, n_pages)
    n_pg = depth * n_pool
    pps = next(c for c in (256, 128, 64, 32, 16, 8, 4, 2, 1) if n_pg % c == 0)

    wp = _prepare(w_in, b_forget, ln_gmlp_g, ln_gmlp_b, w_spatial, b_spatial, w_conv_c, b_conv_c,
                  ln_conv_g, ln_conv_b, w_conv_d, w_branch, w_out, ln1_g, ln1_b, w_router_group,
                  b_router_group, w_router_expert, b_router_expert, w_exp_gate, w_exp_up, w_exp_down,
                  w_ple_gate, w_ple_proj, ln2_g, ln2_b, tm_p, nb, t, pc, pps)

    ck = jnp.transpose(cache_k, (0, 1, 3, 4, 2)).reshape(depth, n_pool, W_BR, PAGE)
    cv = jnp.transpose(cache_v, (0, 1, 3, 4, 2)).reshape(depth, n_pool, W_BR, PAGE)
    lf_t = jnp.transpose(cache_logf, (0, 1, 3, 2))
    lf_rows = jnp.concatenate([lf_t, lf_t], axis=2).reshape(n_pg * SUBLANES, PAGE)
    cc = _page_cumsum(lf_rows, wp["upper"], pps).reshape(depth, n_pool, SUBLANES, LANES)

    pp = p_prompt.reshape(depth, b * s, D_PLE)
    ps = p_sample.reshape(depth, ms, D_PLE)

    xp = x_prompt
    xs = x_sample.reshape(ms, D_MODEL)
    kp, vp, lfp, ccp, cdp = [], [], [], [], []
    ks_, vs_, lfs, ccs, cds, gvs = [], [], [], [], [], []
    for l in range(depth):
        (q, k32, v32, kb, vb, logf, ct, yb, yc, yd, stc, std) = _k1_prompt(l, xp, wp, tm_p)
        ya = _attn_prompt(q, kb, vb, ct, tq, tk)
        f2 = lambda a: a.reshape(b * s, a.shape[-1])
        x1 = _merge(l, f2(xp), f2(ya), f2(yb), f2(yc), f2(yd), wp, tm_p, alpha)
        xp = _moe(l, x1, pp, wp, tm_p, alpha, wp["triu_p"]).reshape(b, s, D_MODEL)
        kp.append(k32.reshape(b, s, HA, DH))
        vp.append(v32.reshape(b, s, HA, DH))
        lfp.append(logf)
        ccp.append(stc[:, HALO_C - (KC - 1):, :])
        cdp.append(std[:, HALO_D - (KD - 1):, :])

        (qs, k_s, v_s, lf_s, vn_s, yb_s, yc_s, yd_s, nsc, nsd) = _k1_sample(
            l, xs, state_conv_c, state_conv_d, wp, nb, t)
        r3 = lambda a: a.reshape(nb, t, a.shape[-1])
        ya_s = _attn_sample(l, page_table, r3(qs), r3(k_s), r3(v_s), r3(lf_s), wp, ck, cv, cc, pc)
        x1s = _merge(l, xs, ya_s.reshape(ms, W_BR), yb_s, yc_s, yd_s, wp, ms, alpha)
        xs = _moe(l, x1s, ps, wp, ms, alpha, wp["triu_s"])
        ks_.append(k_s.reshape(nb, t, HA, DH))
        vs_.append(v_s.reshape(nb, t, HA, DH))
        lfs.append(lf_s[:, 0:HA].reshape(nb, t, HA))
        ccs.append(nsc)
        cds.append(nsd)
        gvs.append(vn_s.reshape(nb, t, GB, DB))

    return (xp, xs.reshape(nb, t, D_MODEL), jnp.stack(kp), jnp.stack(vp), jnp.stack(lfp),
            jnp.stack(ks_), jnp.stack(vs_), jnp.stack(lfs), jnp.stack(ccp), jnp.stack(ccs),
            jnp.stack(cdp), jnp.stack(cds), jnp.stack(gvs))
```

```python
import functools

import numpy as np
import jax
import jax.numpy as jnp
from jax import lax
from jax.experimental import pallas as pl
from jax.experimental.pallas import tpu as pltpu

F32 = jnp.float32
BF16 = jnp.bfloat16

D_MODEL = 1024
W_BR = 256
HA = 4
DH = 64
GB = 4
DB = 64
CHUNK = 128
KC = 31
KD = 3
N_BRANCH = 4
N_GROUPS = 4
EXP_PER_GROUP = 4
N_EXPERTS = 16
D_EXPERT = 256
D_PLE = 256
PAGE = 128
LN_EPS = 1e-5
ATTN_SCALE = DH ** -0.5
LOG2E = 1.4426950408889634
Q_SCALE = ATTN_SCALE * LOG2E
LANES = 128
SUBLANES = 8
HALO_C = 32
HALO_D = 8
W_MIX = 10 * W_BR + LANES
VMEM_LIMIT = 56 * 1024 * 1024


def _cparams(n_axes):
    return pltpu.CompilerParams(dimension_semantics=("arbitrary",) * n_axes,
                                vmem_limit_bytes=VMEM_LIMIT)


def _const_spec(shape, index_map):
    return pl.BlockSpec(shape, index_map, pipeline_mode=pl.Buffered(1))


def _dot(a, b):
    return jnp.dot(a, b, preferred_element_type=F32)


def _dot_nt(a, b):
    return lax.dot_general(a, b, (((1,), (1,)), ((), ())), preferred_element_type=F32)


def _split3(x):
    hi = x.astype(BF16)
    r = x - hi.astype(F32)
    mid = r.astype(BF16)
    lo = (r - mid.astype(F32)).astype(BF16)
    return hi, mid, lo


def _dot3_r(x, m):
    hi, mid, lo = _split3(x)
    return _dot(hi, m) + _dot(mid, m) + _dot(lo, m)


def _dot3_l(m, x):
    hi, mid, lo = _split3(x)
    return _dot(m, hi) + _dot(m, mid) + _dot(m, lo)


def _dot3_nt(m, x):
    hi, mid, lo = _split3(x)
    return _dot_nt(m, hi) + _dot_nt(m, mid) + _dot_nt(m, lo)


def _sigmoid(x):
    return 0.5 * jnp.tanh(0.5 * x) + 0.5


def _silu(x):
    return x * _sigmoid(x)


def _gelu(x):
    return 0.5 * x * (1.0 + jnp.tanh(0.7978845608028654 * (x + 0.044715 * (x * x * x))))


def _log_sigmoid(x):
    return jnp.minimum(x, 0.0) - jnp.log1p(jnp.exp(-jnp.abs(x)))


def _layer_norm(x, g, b):
    mu = jnp.mean(x, axis=-1, keepdims=True)
    d = x - mu
    var = jnp.mean(d * d, axis=-1, keepdims=True)
    return d * lax.rsqrt(var + LN_EPS) * g + b


def _lane_group_mask(rows, width, group, g):
    lane = lax.broadcasted_iota(jnp.int32, (rows, width), 1)
    return (lane >= g * group) & (lane < (g + 1) * group)


def _mixer_front(x, wmix_ref, bf_ref):
    z = _dot(x.astype(BF16), wmix_ref[...])
    seg = lambda i: z[:, i * W_BR:(i + 1) * W_BR]
    f = z[:, 10 * W_BR:10 * W_BR + LANES]
    lane = lax.broadcasted_iota(jnp.int32, f.shape, 1)
    logf = jnp.where(lane < HA, _log_sigmoid(f + bf_ref[...]), 0.0)
    return [seg(i) for i in range(10)], logf


def _gmlp(ub, vb, gavg_ref, lng_ref, lnb_ref, msp_ref, bsp_ref):
    rows = ub.shape[0]
    ug = _gelu(ub)
    vg = _gelu(vb)
    gavg = gavg_ref[...]
    mu = _dot3_r(vg, gavg)
    dv = vg - mu
    var = _dot3_r(dv * dv, gavg)
    vn = dv * lax.rsqrt(var + LN_EPS) * lng_ref[...] + lnb_ref[...]
    vnb = vn.astype(BF16)
    mix = bsp_ref[...]
    for g in range(GB):
        mg = _dot(msp_ref[g], vnb)
        mix = mix + jnp.where(_lane_group_mask(rows, W_BR, DB, g), mg, 0.0)
    return ug * mix, vn


def _k1_prompt_kernel(x_ref, wmix_ref, bf_ref, gavg_ref, lng_ref, lnb_ref, msp_ref, bsp_ref,
                      wc_ref, bc_ref, lcg_ref, lcb_ref, wd_ref, tril_ref,
                      q_ref, k_ref, v_ref, kb_ref, vb_ref, logf_ref, ct_ref,
                      yb_ref, yc_ref, yd_ref, sc_ref, sd_ref,
                      xpc, xpd, carry, xsh, *, tm):
    i = pl.program_id(1)

    @pl.when(i == 0)
    def _():
        xpc[0:HALO_C, :] = jnp.zeros((HALO_C, W_BR), F32)
        xpd[0:HALO_D, :] = jnp.zeros((HALO_D, W_BR), F32)
        carry[...] = jnp.zeros_like(carry)

    (q, k, v, ub, vb, ac, gc, bd, cd, hd), logf = _mixer_front(x_ref[0], wmix_ref, bf_ref)
    q_ref[0] = (q * Q_SCALE).astype(BF16)
    k_ref[0] = k
    v_ref[0] = v
    kb_ref[0] = k.astype(BF16)
    vb_ref[0] = v.astype(BF16)
    logf_ref[0] = logf[:, 0:HA]

    c = _dot3_l(tril_ref[...], logf) + carry[0:1, :]
    carry[0:1, :] = c[tm - 1:tm, :]
    ct_ref[0] = c.T[0:SUBLANES, :]

    yb, _ = _gmlp(ub, vb, gavg_ref, lng_ref, lnb_ref, msp_ref, bsp_ref)
    yb_ref[0] = yb.astype(BF16)

    cin = ac * _sigmoid(gc)
    xpc[HALO_C:HALO_C + tm, :] = cin
    span = tm + HALO_C - SUBLANES
    for r in range(1, SUBLANES):
        xsh[r - 1, 0:span, :] = xpc[r:r + span, :]
    acc = jnp.zeros((tm, W_BR), F32) + bc_ref[...]
    for j in range(KC):
        off = HALO_C - (KC - 1) + j
        r = off % SUBLANES
        a = off - r
        win = xpc[a:a + tm, :] if r == 0 else xsh[r - 1, a:a + tm, :]
        acc = acc + wc_ref[j:j + 1, :] * win
    xpc[0:HALO_C, :] = cin[tm - HALO_C:tm, :]
    sc_ref[0] = cin[tm - HALO_C:tm, :]
    yc_ref[0] = _silu(_layer_norm(acc, lcg_ref[...], lcb_ref[...])).astype(BF16)

    din = cd * hd
    xpd[HALO_D:HALO_D + tm, :] = din
    accd = jnp.zeros((tm, W_BR), F32)
    for j in range(KD):
        off = HALO_D - (KD - 1) + j
        accd = accd + wd_ref[j:j + 1, :] * xpd[off:off + tm, :]
    xpd[0:HALO_D, :] = din[tm - HALO_D:tm, :]
    sd_ref[0] = din[tm - HALO_D:tm, :]
    yd_ref[0] = (bd * accd).astype(BF16)


def _k1_prompt(l, x, wp, tm):
    b, s, _ = x.shape
    nt = s // tm
    row = lambda bb, i: (bb, i, 0)
    cst2 = lambda bb, i: (l, 0, 0)
    o256 = lambda dt: jax.ShapeDtypeStruct((b, s, W_BR), dt)
    in_specs = [
        pl.BlockSpec((1, tm, D_MODEL), row),
        _const_spec((None, D_MODEL, W_MIX), cst2),
        _const_spec((None, 1, LANES), cst2),
        _const_spec((W_BR, W_BR), lambda bb, i: (0, 0)),
        _const_spec((None, 1, W_BR), cst2),
        _const_spec((None, 1, W_BR), cst2),
        _const_spec((None, GB, tm, tm), lambda bb, i: (l, 0, 0, 0)),
        _const_spec((None, tm, W_BR), cst2),
        _const_spec((None, HALO_C, W_BR), cst2),
        _const_spec((None, 1, W_BR), cst2),
        _const_spec((None, 1, W_BR), cst2),
        _const_spec((None, 1, W_BR), cst2),
        _const_spec((None, HALO_D, W_BR), cst2),
        _const_spec((tm, tm), lambda bb, i: (0, 0)),
    ]
    out_shape = [o256(BF16), o256(F32), o256(F32), o256(BF16), o256(BF16),
                 jax.ShapeDtypeStruct((b, s, HA), F32),
                 jax.ShapeDtypeStruct((b, SUBLANES, s), F32),
                 o256(BF16), o256(BF16), o256(BF16),
                 jax.ShapeDtypeStruct((b, HALO_C, W_BR), F32),
                 jax.ShapeDtypeStruct((b, HALO_D, W_BR), F32)]
    blk = pl.BlockSpec((1, tm, W_BR), row)
    out_specs = [blk, blk, blk, blk, blk,
                 pl.BlockSpec((1, tm, HA), row),
                 pl.BlockSpec((1, SUBLANES, tm), lambda bb, i: (bb, 0, i)),
                 blk, blk, blk,
                 pl.BlockSpec((1, HALO_C, W_BR), lambda bb, i: (bb, 0, 0)),
                 pl.BlockSpec((1, HALO_D, W_BR), lambda bb, i: (bb, 0, 0))]
    return pl.pallas_call(
        functools.partial(_k1_prompt_kernel, tm=tm),
        grid=(b, nt), in_specs=in_specs, out_specs=out_specs, out_shape=out_shape,
        scratch_shapes=[pltpu.VMEM((tm + HALO_C, W_BR), F32), pltpu.VMEM((tm + HALO_D, W_BR), F32),
                        pltpu.VMEM((SUBLANES, LANES), F32),
                        pltpu.VMEM((SUBLANES - 1, tm + HALO_C - SUBLANES, W_BR), F32)],
        compiler_params=_cparams(2), name="k1_prompt",
    )(x, wp["wmix"], wp["bf"], wp["gavg"], wp["lng"], wp["lnb"], wp["msp_p"], wp["bsp_p"],
      wp["wc"], wp["bc"], wp["lcg"], wp["lcb"], wp["wd"], wp["tril_p"])


def _k1_sample_kernel(x_ref, wmix_ref, bf_ref, gavg_ref, lng_ref, lnb_ref, msp_ref, bsp_ref,
                      wc_ref, bc_ref, lcg_ref, lcb_ref, wd_ref, stc_ref, std_ref,
                      q_ref, k_ref, v_ref, logf_ref, vn_ref, yb_ref, yc_ref, yd_ref, nsc_ref, nsd_ref,
                      cin_s, ypre_s, din_s, ydpre_s, xs, xsd, *, nb, t):
    (q, k, v, ub, vb, ac, gc, bd, cd, hd), logf = _mixer_front(x_ref[...], wmix_ref, bf_ref)
    q_ref[...] = q * Q_SCALE
    k_ref[...] = k
    v_ref[...] = v
    logf_ref[...] = logf
    yb, vn = _gmlp(ub, vb, gavg_ref, lng_ref, lnb_ref, msp_ref, bsp_ref)
    yb_ref[...] = yb.astype(BF16)
    vn_ref[...] = vn

    cin_s[...] = ac * _sigmoid(gc)
    din_s[...] = cd * hd

    def body(bb, _):
        r0 = pl.multiple_of(bb * t, t)
        xs[0:KC - 1, :] = stc_ref[bb]
        xs[KC - 1:KC - 1 + t, :] = cin_s[pl.ds(r0, t), :]
        acc = jnp.zeros((t, W_BR), F32) + bc_ref[...]
        for j in range(KC):
            acc = acc + wc_ref[j:j + 1, :] * xs[j:j + t, :]
        ypre_s[pl.ds(r0, t), :] = acc
        nsc_ref[bb] = xs[t:t + KC - 1, :]

        xsd[0:KD - 1, :] = std_ref[bb]
        xsd[KD - 1:KD - 1 + t, :] = din_s[pl.ds(r0, t), :]
        accd = jnp.zeros((t, W_BR), F32)
        for j in range(KD):
            accd = accd + wd_ref[j:j + 1, :] * xsd[j:j + t, :]
        ydpre_s[pl.ds(r0, t), :] = accd
        nsd_ref[bb] = xsd[t:t + KD - 1, :]
        return 0

    lax.fori_loop(0, nb, body, 0)
    yc_ref[...] = _silu(_layer_norm(ypre_s[...], lcg_ref[...], lcb_ref[...])).astype(BF16)
    yd_ref[...] = (bd * ydpre_s[...]).astype(BF16)


def _k1_sample(l, x, stc, std, wp, nb, t):
    m = nb * t
    cst2 = lambda i: (l, 0, 0)
    cst4 = lambda i: (l, 0, 0, 0)
    full = lambda shp: pl.BlockSpec(shp, lambda i: (0,) * len(shp))
    in_specs = [
        full((m, D_MODEL)),
        pl.BlockSpec((None, D_MODEL, W_MIX), cst2),
        pl.BlockSpec((None, 1, LANES), cst2),
        full((W_BR, W_BR)),
        pl.BlockSpec((None, 1, W_BR), cst2),
        pl.BlockSpec((None, 1, W_BR), cst2),
        pl.BlockSpec((None, GB, m, m), cst4),
        pl.BlockSpec((None, m, W_BR), cst2),
        pl.BlockSpec((None, HALO_C, W_BR), cst2),
        pl.BlockSpec((None, 1, W_BR), cst2),
        pl.BlockSpec((None, 1, W_BR), cst2),
        pl.BlockSpec((None, 1, W_BR), cst2),
        pl.BlockSpec((None, HALO_D, W_BR), cst2),
        pl.BlockSpec((None, nb, KC - 1, W_BR), cst4),
        pl.BlockSpec((None, nb, KD - 1, W_BR), cst4),
    ]
    o = lambda dt: jax.ShapeDtypeStruct((m, W_BR), dt)
    out_shape = [o(F32), o(F32), o(F32), jax.ShapeDtypeStruct((m, LANES), F32), o(F32),
                 o(BF16), o(BF16), o(BF16),
                 jax.ShapeDtypeStruct((nb, KC - 1, W_BR), F32),
                 jax.ShapeDtypeStruct((nb, KD - 1, W_BR), F32)]
    out_specs = [full(s.shape) for s in out_shape]
    return pl.pallas_call(
        functools.partial(_k1_sample_kernel, nb=nb, t=t),
        grid=(1,), in_specs=in_specs, out_specs=out_specs, out_shape=out_shape,
        scratch_shapes=[pltpu.VMEM((m, W_BR), F32), pltpu.VMEM((m, W_BR), F32),
                        pltpu.VMEM((m, W_BR), F32), pltpu.VMEM((m, W_BR), F32),
                        pltpu.VMEM((KC - 1 + t + 2, W_BR), F32), pltpu.VMEM((2 * SUBLANES, W_BR), F32)],
        compiler_params=pltpu.CompilerParams(vmem_limit_bytes=VMEM_LIMIT), name="k1_sample",
    )(x, wp["wmix"], wp["bf"], wp["gavg"], wp["lng"], wp["lnb"], wp["msp_s"], wp["bsp_s"],
      wp["wc"], wp["bc"], wp["lcg"], wp["lcb"], wp["wd"], stc, std)


def _attn_prompt_kernel(qi_tab, ki_tab, q_ref, k_ref, v_ref, ck_ref, cq_ref, o_ref,
                        qm_s, m_s, acc_s, *, tq, tk):
    p = pl.program_id(1)
    qi = qi_tab[p]
    ki = ki_tab[p]
    last_k = ((qi + 1) * tq - 1) // tk

    @pl.when(ki == 0)
    def _():
        q = q_ref[0]
        for h in range(HA):
            q_pair = q[:, (h // 2) * LANES:(h // 2 + 1) * LANES]
            qm_s[h] = jnp.where(_lane_group_mask(tq, LANES, DH, h % 2), q_pair, jnp.zeros_like(q_pair))
        m_s[...] = jnp.full(m_s.shape, -jnp.inf, F32)
        acc_s[...] = jnp.zeros(acc_s.shape, F32)

    def step(masked, nk):
        k = k_ref[0, 0:nk, :]
        v = v_ref[0, 0:nk, :]
        ck = ck_ref[0, :, 0:nk]
        cq = cq_ref[0]
        if masked:
            keep = ((ki * tk - qi * tq) + lax.broadcasted_iota(jnp.int32, (tq, nk), 1)
                    <= lax.broadcasted_iota(jnp.int32, (tq, nk), 0))
        for h in range(HA):
            bias = (cq[h:h + 1, 0:1] - ck[h:h + 1, :]) * LOG2E
            s = _dot_nt(qm_s[h], k[:, (h // 2) * LANES:(h // 2 + 1) * LANES]) + bias
            if masked:
                s = jnp.where(keep, s, -jnp.inf)
            m_prev = m_s[h]
            m_new = jnp.maximum(m_prev, jnp.max(s, axis=1, keepdims=True))
            alpha = jnp.exp2(m_prev - m_new)
            pr = jnp.exp2(s - m_new).astype(BF16)
            v_h = jnp.where(_lane_group_mask(nk, W_BR, DH, h), v, jnp.ones_like(v))
            acc_s[h] = acc_s[h] * alpha + _dot(pr, v_h)
            m_s[h] = m_new

    n_visible = (qi + 1) * tq - ki * tk
    half = tk // 2

    @pl.when(n_visible >= tk + tq)
    def _():
        step(False, tk)

    @pl.when((n_visible < tk + tq) & (n_visible > half))
    def _():
        step(True, tk)

    @pl.when(n_visible <= half)
    def _():
        step(True, half)

    @pl.when(ki == last_k)
    def _():
        out = jnp.zeros((tq, W_BR), F32)
        for h in range(HA):
            acc = acc_s[h]
            den_lane = ((h + 1) % HA) * DH
            out = out + jnp.where(_lane_group_mask(tq, W_BR, DH, h), acc / acc[:, den_lane:den_lane + 1], 0.0)
        o_ref[0] = out.astype(BF16)


def _attn_prompt(q, k, v, ct, tq, tk):
    b, s, _ = q.shape
    nq = s // tq
    n_k = [((i + 1) * tq - 1) // tk + 1 for i in range(nq)]
    qi_np = np.concatenate([np.full(n_k[i], i, np.int32) for i in range(nq)])
    ki_np = np.concatenate([np.arange(n_k[i], dtype=np.int32) for i in range(nq)])
    n_pairs = int(qi_np.shape[0])
    grid_spec = pltpu.PrefetchScalarGridSpec(
        num_scalar_prefetch=2, grid=(b, n_pairs),
        in_specs=[
            pl.BlockSpec((1, tq, W_BR), lambda bb, p, qt, kt: (bb, qt[p], 0)),
            pl.BlockSpec((1, tk, W_BR), lambda bb, p, qt, kt: (bb, kt[p], 0)),
            pl.BlockSpec((1, tk, W_BR), lambda bb, p, qt, kt: (bb, kt[p], 0)),
            pl.BlockSpec((1, SUBLANES, tk), lambda bb, p, qt, kt: (bb, 0, kt[p])),
            pl.BlockSpec((1, SUBLANES, tq), lambda bb, p, qt, kt: (bb, 0, qt[p])),
        ],
        out_specs=pl.BlockSpec((1, tq, W_BR), lambda bb, p, qt, kt: (bb, qt[p], 0)),
        scratch_shapes=[pltpu.VMEM((HA, tq, LANES), BF16), pltpu.VMEM((HA, tq, 1), F32),
                        pltpu.VMEM((HA, tq, W_BR), F32)],
    )
    return pl.pallas_call(
        functools.partial(_attn_prompt_kernel, tq=tq, tk=tk),
        grid_spec=grid_spec, out_shape=jax.ShapeDtypeStruct((b, s, W_BR), BF16),
        compiler_params=_cparams(2), name="attn_prompt",
    )(jnp.asarray(qi_np), jnp.asarray(ki_np), q, k, v, ct, ct)


def _page_cumsum_kernel(lf_ref, u_ref, o_ref):
    o_ref[...] = _dot3_r(lf_ref[...], u_ref[...])


def _page_cumsum(lf_rows, upper, pps):
    n_rows = lf_rows.shape[0]
    return pl.pallas_call(
        _page_cumsum_kernel, grid=(n_rows // (pps * SUBLANES),),
        in_specs=[pl.BlockSpec((pps * SUBLANES, PAGE), lambda i: (i, 0)),
                  _const_spec((PAGE, PAGE), lambda i: (0, 0))],
        out_specs=pl.BlockSpec((pps * SUBLANES, PAGE), lambda i: (i, 0)),
        out_shape=jax.ShapeDtypeStruct((n_rows, PAGE), F32),
        compiler_params=_cparams(1), name="page_cumsum",
    )(lf_rows, upper)


def _attn_sample_kernel(pt_ref, q_ref, kn_ref, vn_ref, lfn_ref, pmat_ref, a8_ref, ltri_ref, rsel_ref,
                        ck_hbm, cv_hbm, cc_hbm, o_ref, kbuf, vbuf, cbuf, sem, m_s, l_s, acc_s, carry_s,
                        *, l, nb, t, n_pages, pc):
    b = pl.program_id(0)
    c = pl.program_id(1)
    nc = n_pages // pc
    step_id = b * nc + c
    slot = step_id % 2
    rows = HA * t
    ng = rows // SUBLANES

    def copies(bb, cc, sl, p):
        pg = pt_ref[bb, cc * pc + p]
        return (pltpu.make_async_copy(ck_hbm.at[l, pg], kbuf.at[sl, p], sem.at[sl, 0]),
                pltpu.make_async_copy(cv_hbm.at[l, pg], vbuf.at[sl, p], sem.at[sl, 1]),
                pltpu.make_async_copy(cc_hbm.at[l, pg], cbuf.at[sl, p], sem.at[sl, 2]))

    def issue(bb, cc, sl):
        def body(p, _):
            for cp in copies(bb, cc, sl, p):
                cp.start()
            return 0
        lax.fori_loop(0, pc, body, 0)

    @pl.when(step_id == 0)
    def _():
        issue(0, 0, 0)

    @pl.when(step_id + 1 < nb * nc)
    def _():
        nxt = step_id + 1
        issue(nxt // nc, nxt % nc, 1 - slot)

    def wait_body(p, _):
        for cp in copies(b, c, slot, p):
            cp.wait()
        return 0
    lax.fori_loop(0, pc, wait_body, 0)

    @pl.when(c == 0)
    def _():
        m_s[...] = jnp.full(m_s.shape, -jnp.inf, F32)
        l_s[...] = jnp.zeros(l_s.shape, F32)
        acc_s[...] = jnp.zeros(acc_s.shape, F32)
        carry_s[...] = jnp.zeros(carry_s.shape, F32)

    q = q_ref[0]
    sub = lax.broadcasted_iota(jnp.int32, (SUBLANES, W_BR), 0)
    lane = lax.broadcasted_iota(jnp.int32, (SUBLANES, W_BR), 1)
    head_ok = (lane // DH) == (sub % HA)
    qm = []
    for g in range(t // 2):
        qa = jnp.broadcast_to(q[2 * g:2 * g + 1, :], (SUBLANES, W_BR))
        qb = jnp.broadcast_to(q[2 * g + 1:2 * g + 2, :], (SUBLANES, W_BR))
        qm.append(jnp.where(head_ok, jnp.where(sub < HA, qa, qb), 0.0))
    qm = jnp.concatenate(qm, axis=0).astype(BF16)

    def online_update(s3, pv_fn):
        m_prev = m_s[...]
        m_new = jnp.maximum(m_prev, jnp.max(s3, axis=2, keepdims=True))
        alpha = jnp.exp2(m_prev - m_new)
        pr = jnp.exp2(s3 - m_new)
        l_s[...] = alpha * l_s[...] + jnp.sum(pr, axis=2, keepdims=True)
        acc_s[...] = acc_s[...] * alpha.reshape(rows, 1) + pv_fn(pr.reshape(rows, s3.shape[2]).astype(BF16))
        m_s[...] = m_new

    c2 = cbuf[slot].reshape(pc * SUBLANES, LANES)
    off = _dot3_l(pmat_ref[...], c2)
    carry = jnp.concatenate([carry_s[:, 0:1]] * pc, axis=0)
    cf = c2 + off[:, LANES - 1:LANES] + carry
    tot = cf[(pc - 1) * SUBLANES:pc * SUBLANES, LANES - 1:LANES]
    carry_s[...] = jnp.broadcast_to(tot, carry_s.shape)
    c_past = jnp.concatenate([cf[j * SUBLANES:(j + 1) * SUBLANES, :] for j in range(pc)], axis=1)

    s_past = jnp.concatenate([_dot(qm, kbuf[slot, j].astype(BF16)) for j in range(pc)], axis=1)
    s_past = s_past.reshape(ng, SUBLANES, pc * PAGE) - (c_past * LOG2E)[None]

    def pv_past(p2):
        acc = jnp.zeros((rows, W_BR), F32)
        for j in range(pc):
            acc = acc + _dot_nt(p2[:, j * PAGE:(j + 1) * PAGE], vbuf[slot, j].astype(BF16))
        return acc

    online_update(s_past, pv_past)

    @pl.when(c == nc - 1)
    def _():
        lfp = jnp.concatenate([lfn_ref[0], jnp.zeros((LANES - t, LANES), F32)], axis=0)
        cum_new = _dot3_l(ltri_ref[...], lfp)
        c_new = _dot3_nt(a8_ref[...], cum_new) + tot
        zpad = jnp.zeros((LANES - t, W_BR), F32)
        kn = jnp.concatenate([kn_ref[0], zpad], axis=0).astype(BF16)
        vn = jnp.concatenate([vn_ref[0], zpad], axis=0).astype(BF16)
        s_new = _dot_nt(qm, kn).reshape(ng, SUBLANES, LANES) - (c_new * LOG2E)[None]
        gi = lax.broadcasted_iota(jnp.int32, s_new.shape, 0)
        si = lax.broadcasted_iota(jnp.int32, s_new.shape, 1)
        ji = lax.broadcasted_iota(jnp.int32, s_new.shape, 2)
        s_new = jnp.where(ji <= 2 * gi + si // HA, s_new, -jnp.inf)
        online_update(s_new, lambda p2: _dot(p2, vn))

        o = acc_s[...] / l_s[...].reshape(rows, 1)
        sub_r = lax.broadcasted_iota(jnp.int32, (rows, W_BR), 0)
        lane_r = lax.broadcasted_iota(jnp.int32, (rows, W_BR), 1)
        om = jnp.where((lane_r // DH) == (sub_r % HA), o, 0.0)
        o_ref[0] = _dot(rsel_ref[...], om.astype(BF16))


def _attn_sample(l, page_table, q, kn, vn, lfn, wp, ck, cv, cc, pc):
    nb, t, _ = q.shape
    n_pages = page_table.shape[1]
    rows = HA * t
    cst = lambda shp: pl.BlockSpec(shp, lambda b, c, pt: (0,) * len(shp))
    per_b = lambda w: pl.BlockSpec((1, t, w), lambda b, c, pt: (b, 0, 0))
    grid_spec = pltpu.PrefetchScalarGridSpec(
        num_scalar_prefetch=1, grid=(nb, n_pages // pc),
        in_specs=[per_b(W_BR), per_b(W_BR), per_b(W_BR), per_b(LANES),
                  cst((pc * SUBLANES, pc * SUBLANES)), cst((SUBLANES, LANES)),
                  cst((LANES, LANES)), cst((t, rows)),
                  pl.BlockSpec(memory_space=pl.ANY), pl.BlockSpec(memory_space=pl.ANY),
                  pl.BlockSpec(memory_space=pl.ANY)],
        out_specs=per_b(W_BR),
        scratch_shapes=[pltpu.VMEM((2, pc, W_BR, PAGE), F32),
                        pltpu.VMEM((2, pc, W_BR, PAGE), F32),
                        pltpu.VMEM((2, pc, SUBLANES, LANES), F32),
                        pltpu.SemaphoreType.DMA((2, 3)),
                        pltpu.VMEM((rows // SUBLANES, SUBLANES, 1), F32),
                        pltpu.VMEM((rows // SUBLANES, SUBLANES, 1), F32),
                        pltpu.VMEM((rows, W_BR), F32),
                        pltpu.VMEM((SUBLANES, LANES), F32)],
    )
    return pl.pallas_call(
        functools.partial(_attn_sample_kernel, l=l, nb=nb, t=t, n_pages=n_pages, pc=pc),
        grid_spec=grid_spec, out_shape=jax.ShapeDtypeStruct((nb, t, W_BR), F32),
        compiler_params=_cparams(2), name="attn_sample",
    )(page_table, q, kn, vn, lfn, wp["pmat"], wp["a8"], wp["ltri"], wp["rsel"], ck, cv, cc)


def _merge_kernel(x_ref, ya_ref, yb_ref, yc_ref, yd_ref, wg_ref, wbr_ref, wo_ref, g_ref, b_ref, o_ref,
                  *, alpha):
    x = x_ref[...]
    xb = x.astype(BF16)
    ys = (ya_ref, yb_ref, yc_ref, yd_ref)
    acc = jnp.zeros(x.shape, F32)
    for n in range(N_BRANCH):
        gate = _sigmoid(_dot(xb, wg_ref[:, n * D_MODEL:(n + 1) * D_MODEL]))
        acc = acc + gate * _dot(ys[n][...].astype(BF16), wbr_ref[n])
    mix = _dot(acc.astype(BF16), wo_ref[...])
    o_ref[...] = _layer_norm(alpha * x + mix, g_ref[...], b_ref[...])


def _merge(l, x, ya, yb, yc, yd, wp, tm, alpha):
    m = x.shape[0]
    row = lambda w: pl.BlockSpec((tm, w), lambda i: (i, 0))
    c3 = lambda i: (l, 0, 0)
    return pl.pallas_call(
        functools.partial(_merge_kernel, alpha=alpha),
        grid=(m // tm,),
        in_specs=[row(D_MODEL), row(W_BR), row(W_BR), row(W_BR), row(W_BR),
                  _const_spec((None, D_MODEL, N_BRANCH * D_MODEL), c3),
                  _const_spec((None, N_BRANCH, W_BR, D_MODEL), lambda i: (l, 0, 0, 0)),
                  _const_spec((None, D_MODEL, D_MODEL), c3),
                  _const_spec((None, 1, D_MODEL), c3), _const_spec((None, 1, D_MODEL), c3)],
        out_specs=row(D_MODEL), out_shape=jax.ShapeDtypeStruct((m, D_MODEL), F32),
        compiler_params=_cparams(1), name="merge",
    )(x, ya, yb, yc, yd, wp["wgate"], wp["wbr"], wp["wout"], wp["ln1g"], wp["ln1b"])


def _route_tokens_on_lanes(lt):
    sub = lax.broadcasted_iota(jnp.int32, lt.shape, 0)
    big = jnp.int32(LANES)
    lg = jnp.where(sub < N_GROUPS, lt, -jnp.inf)
    mg = jnp.max(lg, axis=0, keepdims=True)
    p_top = 1.0 / jnp.sum(jnp.exp(lg - mg), axis=0, keepdims=True)
    g_idx = jnp.min(jnp.where(lg == mg, sub, big), axis=0, keepdims=True)
    lo = N_GROUPS + g_idx * EXP_PER_GROUP
    le = jnp.where((sub >= lo) & (sub < lo + EXP_PER_GROUP), lt, -jnp.inf)
    v1 = jnp.max(le, axis=0, keepdims=True)
    i1 = jnp.min(jnp.where(le == v1, sub, big), axis=0, keepdims=True)
    le2 = jnp.where(sub == i1, -jnp.inf, le)
    v2 = jnp.max(le2, axis=0, keepdims=True)
    i2 = jnp.min(jnp.where(le2 == v2, sub, big), axis=0, keepdims=True)
    e2 = jnp.exp(v2 - v1)
    w1 = p_top / (1.0 + e2)
    w2 = p_top * e2 / (1.0 + e2)
    comb_t = jnp.where(sub == i1, w1, 0.0) + jnp.where(sub == i2, w2, 0.0)
    return g_idx, comb_t


def _moe_kernel(x_ref, ple_ref, wrh_ref, wrl_ref, br_ref, triu_ref, wg_ref, wu_ref, wd_ref, wpg_ref, wpp_ref,
                g_ref, b_ref, o_ref, moe_s, *, alpha, cap):
    x = x_ref[...]
    tm = x.shape[0]
    xb = x.astype(BF16)
    xm = (x - xb.astype(F32)).astype(BF16)
    nr = 3 * SUBLANES
    kpad = -(-cap // LANES) * LANES

    wrh = wrh_ref[...]
    lt = (_dot_nt(wrh, xb) + _dot_nt(wrl_ref[...], xb) + _dot_nt(wrh, xm) + br_ref[...])[0:nr]
    g_idx, comb_t = _route_tokens_on_lanes(lt)

    sub16 = lax.broadcasted_iota(jnp.int32, (2 * SUBLANES, tm), 0)
    onehot = jnp.where(sub16 == g_idx, 1.0, 0.0)
    before = _dot(onehot.astype(BF16), triu_ref[...])
    rank = jnp.sum(onehot * before, axis=0, keepdims=True)
    n_max = jnp.max(before[:, tm - 1:tm] + onehot[:, tm - 1:tm])

    sub8 = lax.broadcasted_iota(jnp.int32, (SUBLANES, tm), 0)
    extra = jnp.where(sub8 == 0, rank, jnp.where(sub8 == 1, g_idx.astype(F32), 0.0))
    col = jnp.concatenate([comb_t, extra, jnp.zeros((LANES - nr - SUBLANES, tm), F32)], axis=0).T
    rank_c = col[:, nr:nr + 1].astype(jnp.int32)
    g_c = col[:, nr + 1:nr + 2].astype(jnp.int32)

    def experts(rows_b, comb_rows, e_list):
        y = jnp.zeros((rows_b.shape[0], D_MODEL), F32)
        for e in e_list:
            hg = _dot(rows_b, wg_ref[e])
            hu = _dot(rows_b, wu_ref[e])
            hh = _silu(hg) * hu * comb_rows[:, N_GROUPS + e:N_GROUPS + e + 1]
            y = y + _dot(hh.astype(BF16), wd_ref[e])
        return y

    @pl.when(n_max <= cap)
    def _():
        comb_hi = col.astype(BF16)
        comb_lo = (col - comb_hi.astype(F32)).astype(BF16)
        r_iota = lax.broadcasted_iota(jnp.int32, (cap, tm), 0)
        c_iota = lax.broadcasted_iota(jnp.int32, (tm, kpad), 1)
        rank_i = rank.astype(jnp.int32)
        acc = jnp.zeros((tm, D_MODEL), F32)
        for g in range(N_GROUPS):
            pg = jnp.where((r_iota == rank_i) & (g_idx == g), 1.0, 0.0).astype(BF16)
            pgt = jnp.where((c_iota == rank_c) & (g_c == g), 1.0, 0.0).astype(BF16)
            xs = _dot(pg, xb).astype(BF16)
            cs = _dot(pg, comb_hi) + _dot(pg, comb_lo)
            y = experts(xs, cs, range(g * EXP_PER_GROUP, (g + 1) * EXP_PER_GROUP))
            if kpad > cap:
                y = jnp.concatenate([y, jnp.zeros((kpad - cap, D_MODEL), F32)], axis=0)
            acc = acc + _dot(pgt, y.astype(BF16))
        moe_s[...] = acc

    @pl.when(n_max > cap)
    def _():
        moe_s[...] = experts(xb, col, range(N_EXPERTS))

    ple_term = _sigmoid(_dot(xb, wpg_ref[...])) * _dot(ple_ref[...].astype(BF16), wpp_ref[...])
    o_ref[...] = _layer_norm(alpha * x + moe_s[...] + ple_term, g_ref[...], b_ref[...])


def _moe(l, x, ple, wp, tm, alpha, triu):
    m = x.shape[0]
    cap = (tm * 5) // 16
    c3 = lambda i: (l, 0, 0)
    c4 = lambda i: (l, 0, 0, 0)
    return pl.pallas_call(
        functools.partial(_moe_kernel, alpha=alpha, cap=cap),
        grid=(m // tm,),
        in_specs=[pl.BlockSpec((tm, D_MODEL), lambda i: (i, 0)),
                  pl.BlockSpec((None, tm, D_PLE), lambda i: (l, i, 0)),
                  _const_spec((None, LANES, D_MODEL), c3), _const_spec((None, LANES, D_MODEL), c3),
                  _const_spec((None, LANES, 1), c3),
                  _const_spec((tm, tm), lambda i: (0, 0)),
                  _const_spec((None, N_EXPERTS, D_MODEL, D_EXPERT), c4),
                  _const_spec((None, N_EXPERTS, D_MODEL, D_EXPERT), c4),
                  _const_spec((None, N_EXPERTS, D_EXPERT, D_MODEL), c4),
                  _const_spec((None, D_MODEL, D_MODEL), c3), _const_spec((None, D_PLE, D_MODEL), c3),
                  _const_spec((None, 1, D_MODEL), c3), _const_spec((None, 1, D_MODEL), c3)],
        out_specs=pl.BlockSpec((tm, D_MODEL), lambda i: (i, 0)),
        out_shape=jax.ShapeDtypeStruct((m, D_MODEL), F32),
        scratch_shapes=[pltpu.VMEM((tm, D_MODEL), F32)],
        compiler_params=_cparams(1), name="moe",
    )(x, ple, wp["wr_hi"], wp["wr_lo"], wp["br"], triu, wp["weg"], wp["weu"], wp["wed"],
      wp["wpg"], wp["wpp"], wp["ln2g"], wp["ln2b"])


def _kron_tril(w_spatial, n_blocks, n):
    w = jnp.tril(w_spatial)[:, :, :n, :n]
    eye = jnp.eye(n_blocks, dtype=w.dtype)
    out = jnp.einsum("ab,lgij->lgaibj", eye, w)
    d = w_spatial.shape[0]
    return out.reshape(d, GB, n_blocks * n, n_blocks * n).astype(BF16)


def _prepare(w_in, b_forget, ln_gmlp_g, ln_gmlp_b, w_spatial, b_spatial, w_conv_c, b_conv_c, ln_conv_g,
             ln_conv_b, w_conv_d, w_branch, w_out, ln1_g, ln1_b, w_router_group, b_router_group,
             w_router_expert, b_router_expert, w_exp_gate, w_exp_up, w_exp_down, w_ple_gate, w_ple_proj,
             ln2_g, ln2_b, tm_p, nb, t, pc, pps):
    d = w_in.shape[0]
    o_f = 3 * W_BR
    o_rest = o_f + HA
    o_gate = o_rest + 7 * W_BR
    r3 = lambda a: a.reshape(d, 1, -1)
    wp = {}
    wp["wmix"] = jnp.concatenate(
        [w_in[:, :, 0:o_f], w_in[:, :, o_rest:o_gate],
         jnp.pad(w_in[:, :, o_f:o_rest], ((0, 0), (0, 0), (0, LANES - HA)))], axis=-1).astype(BF16)
    wp["wgate"] = w_in[:, :, o_gate:].astype(BF16)
    wp["bf"] = jnp.pad(b_forget, ((0, 0), (0, LANES - HA))).reshape(d, 1, LANES)
    gi = np.arange(W_BR) // DB
    wp["gavg"] = jnp.asarray((gi[:, None] == gi[None, :]).astype(np.float32) / DB, dtype=BF16)
    wp["lng"] = r3(ln_gmlp_g)
    wp["lnb"] = r3(ln_gmlp_b)
    wp["msp_p"] = _kron_tril(w_spatial, tm_p // CHUNK, CHUNK)
    wp["msp_s"] = _kron_tril(w_spatial, nb, t)
    bsp = jnp.repeat(jnp.swapaxes(b_spatial, 1, 2), DB, axis=2)
    wp["bsp_p"] = jnp.tile(bsp, (1, tm_p // CHUNK, 1))
    wp["bsp_s"] = jnp.tile(bsp[:, :t, :], (1, nb, 1))
    wp["wc"] = jnp.pad(w_conv_c, ((0, 0), (0, HALO_C - KC), (0, 0)))
    wp["bc"] = r3(b_conv_c)
    wp["lcg"] = r3(ln_conv_g)
    wp["lcb"] = r3(ln_conv_b)
    wp["wd"] = jnp.pad(w_conv_d, ((0, 0), (0, HALO_D - KD), (0, 0)))
    wp["tril_p"] = jnp.asarray(np.tril(np.ones((tm_p, tm_p), np.float32)), dtype=BF16)
    wp["wbr"] = w_branch.astype(BF16)
    wp["wout"] = w_out.astype(BF16)
    wp["ln1g"] = r3(ln1_g)
    wp["ln1b"] = r3(ln1_b)
    wr = jnp.swapaxes(jnp.concatenate([w_router_group, w_router_expert,
                                       jnp.zeros((d, D_MODEL, LANES - N_GROUPS - N_EXPERTS), F32)], axis=-1), 1, 2)
    wr_hi = wr.astype(BF16)
    wp["wr_hi"] = wr_hi
    wp["wr_lo"] = (wr - wr_hi.astype(F32)).astype(BF16)
    wp["br"] = jnp.concatenate([b_router_group, b_router_expert,
                                jnp.zeros((d, LANES - N_GROUPS - N_EXPERTS), F32)], axis=-1).reshape(d, LANES, 1)
    strict_upper = lambda n: jnp.asarray(np.triu(np.ones((n, n), np.float32), 1), dtype=BF16)
    wp["triu_p"] = strict_upper(tm_p)
    wp["triu_s"] = strict_upper(nb * t)
    wp["weg"] = w_exp_gate.astype(BF16)
    wp["weu"] = w_exp_up.astype(BF16)
    wp["wed"] = w_exp_down.astype(BF16)
    wp["wpg"] = w_ple_gate.astype(BF16)
    wp["wpp"] = w_ple_proj.astype(BF16)
    wp["ln2g"] = r3(ln2_g)
    wp["ln2b"] = r3(ln2_b)

    rows = pc * SUBLANES
    ri = np.arange(rows)
    pm = ((ri[:, None] % SUBLANES) == (ri[None, :] % SUBLANES)) & ((ri[None, :] // SUBLANES) < (ri[:, None] // SUBLANES))
    wp["pmat"] = jnp.asarray(pm.astype(np.float32), dtype=BF16)
    a8 = (np.arange(LANES)[None, :] == (np.arange(SUBLANES)[:, None] % HA)).astype(np.float32)
    wp["a8"] = jnp.asarray(a8, dtype=BF16)
    wp["ltri"] = jnp.asarray(np.tril(np.ones((LANES, LANES), np.float32)), dtype=BF16)
    rsel = ((np.arange(HA * t)[None, :] // HA) == np.arange(t)[:, None]).astype(np.float32)
    wp["rsel"] = jnp.asarray(rsel, dtype=BF16)
    wp["upper"] = jnp.asarray(np.triu(np.ones((PAGE, PAGE), np.float32)), dtype=BF16)
    return wp


def kernel(x_prompt, x_sample, cache_k, cache_v, cache_logf, state_conv_c, state_conv_d, page_table,
           p_prompt, p_sample, w_in, b_forget, ln_gmlp_g, ln_gmlp_b, w_spatial, b_spatial, w_conv_c,
           b_conv_c, ln_conv_g, ln_conv_b, w_conv_d, w_branch, w_out, ln1_g, ln1_b, w_router_group,
           b_router_group, w_router_expert, b_router_expert, w_exp_gate, w_exp_up, w_exp_down,
           w_ple_gate, w_ple_proj, ln2_g, ln2_b):
    depth = w_in.shape[0]
    alpha = float((2 * depth) ** 0.25)
    b, s, _ = x_prompt.shape
    nb, t, _ = x_sample.shape
    n_pool = cache_k.shape[1]
    n_pages = page_table.shape[1]
    tm_p = min(512, s)
    tq, tk = min(512, s), min(1024, s)
    ms = nb * t
    pc = min(64, n_pages)
    n_pg = depth * n_pool
    pps = next(c for c in (256, 128, 64, 32, 16, 8, 4, 2, 1) if n_pg % c == 0)

    wp = _prepare(w_in, b_forget, ln_gmlp_g, ln_gmlp_b, w_spatial, b_spatial, w_conv_c, b_conv_c,
                  ln_conv_g, ln_conv_b, w_conv_d, w_branch, w_out, ln1_g, ln1_b, w_router_group,
                  b_router_group, w_router_expert, b_router_expert, w_exp_gate, w_exp_up, w_exp_down,
                  w_ple_gate, w_ple_proj, ln2_g, ln2_b, tm_p, nb, t, pc, pps)

    ck = jnp.transpose(cache_k, (0, 1, 3, 4, 2)).reshape(depth, n_pool, W_BR, PAGE)
    cv = jnp.transpose(cache_v, (0, 1, 3, 4, 2)).reshape(depth, n_pool, W_BR, PAGE)
    lf_t = jnp.transpose(cache_logf, (0, 1, 3, 2))
    lf_rows = jnp.concatenate([lf_t, lf_t], axis=2).reshape(n_pg * SUBLANES, PAGE)
    cc = _page_cumsum(lf_rows, wp["upper"], pps).reshape(depth, n_pool, SUBLANES, LANES)

    pp = p_prompt.reshape(depth, b * s, D_PLE)
    ps = p_sample.reshape(depth, ms, D_PLE)

    xp = x_prompt
    xs = x_sample.reshape(ms, D_MODEL)
    kp, vp, lfp, ccp, cdp = [], [], [], [], []
    ks_, vs_, lfs, ccs, cds, gvs = [], [], [], [], [], []
    for l in range(depth):
        (q, k32, v32, kb, vb, logf, ct, yb, yc, yd, stc, std) = _k1_prompt(l, xp, wp, tm_p)
        ya = _attn_prompt(q, kb, vb, ct, tq, tk)
        f2 = lambda a: a.reshape(b * s, a.shape[-1])
        x1 = _merge(l, f2(xp), f2(ya), f2(yb), f2(yc), f2(yd), wp, tm_p, alpha)
        xp = _moe(l, x1, pp, wp, tm_p, alpha, wp["triu_p"]).reshape(b, s, D_MODEL)
        kp.append(k32.reshape(b, s, HA, DH))
        vp.append(v32.reshape(b, s, HA, DH))
        lfp.append(logf)
        ccp.append(stc[:, HALO_C - (KC - 1):, :])
        cdp.append(std[:, HALO_D - (KD - 1):, :])

        (qs, k_s, v_s, lf_s, vn_s, yb_s, yc_s, yd_s, nsc, nsd) = _k1_sample(
            l, xs, state_conv_c, state_conv_d, wp, nb, t)
        r3 = lambda a: a.reshape(nb, t, a.shape[-1])
        ya_s = _attn_sample(l, page_table, r3(qs), r3(k_s), r3(v_s), r3(lf_s), wp, ck, cv, cc, pc)
        x1s = _merge(l, xs, ya_s.reshape(ms, W_BR), yb_s, yc_s, yd_s, wp, ms, alpha)
        xs = _moe(l, x1s, ps, wp, ms, alpha, wp["triu_s"])
        ks_.append(k_s.reshape(nb, t, HA, DH))
        vs_.append(v_s.reshape(nb, t, HA, DH))
        lfs.append(lf_s[:, 0:HA].reshape(nb, t, HA))
        ccs.append(nsc)
        cds.append(nsd)
        gvs.append(vn_s.reshape(nb, t, GB, DB))

    return (xp, xs.reshape(nb, t, D_MODEL), jnp.stack(kp), jnp.stack(vp), jnp.stack(lfp),
            jnp.stack(ks_), jnp.stack(vs_), jnp.stack(lfs), jnp.stack(ccp), jnp.stack(ccs),
            jnp.stack(cdp), jnp.stack(cds), jnp.stack(gvs))
```

```python
import functools

import numpy as np
import jax
import jax.numpy as jnp
from jax import lax
from jax.experimental import pallas as pl
from jax.experimental.pallas import tpu as pltpu

F32 = jnp.float32
BF16 = jnp.bfloat16

D_MODEL = 1024
W_BR = 256
HA = 4
DH = 64
GB = 4
DB = 64
CHUNK = 128
KC = 31
KD = 3
N_BRANCH = 4
N_GROUPS = 4
EXP_PER_GROUP = 4
N_EXPERTS = 16
D_EXPERT = 256
D_PLE = 256
PAGE = 128
LN_EPS = 1e-5
ATTN_SCALE = DH ** -0.5
LOG2E = 1.4426950408889634
Q_SCALE = ATTN_SCALE * LOG2E
LANES = 128
SUBLANES = 8
HALO_C = 32
HALO_D = 8
W_MIX = 10 * W_BR + LANES
VMEM_LIMIT = 56 * 1024 * 1024


def _cparams(n_axes):
    return pltpu.CompilerParams(dimension_semantics=("arbitrary",) * n_axes,
                                vmem_limit_bytes=VMEM_LIMIT)


def _const_spec(shape, index_map):
    return pl.BlockSpec(shape, index_map, pipeline_mode=pl.Buffered(1))


def _dot(a, b):
    return jnp.dot(a, b, preferred_element_type=F32)


def _dot_nt(a, b):
    return lax.dot_general(a, b, (((1,), (1,)), ((), ())), preferred_element_type=F32)


def _split3(x):
    hi = x.astype(BF16)
    r = x - hi.astype(F32)
    mid = r.astype(BF16)
    lo = (r - mid.astype(F32)).astype(BF16)
    return hi, mid, lo


def _dot3_r(x, m):
    hi, mid, lo = _split3(x)
    return _dot(hi, m) + _dot(mid, m) + _dot(lo, m)


def _dot3_l(m, x):
    hi, mid, lo = _split3(x)
    return _dot(m, hi) + _dot(m, mid) + _dot(m, lo)


def _dot3_nt(m, x):
    hi, mid, lo = _split3(x)
    return _dot_nt(m, hi) + _dot_nt(m, mid) + _dot_nt(m, lo)


def _sigmoid(x):
    return 0.5 * jnp.tanh(0.5 * x) + 0.5


def _silu(x):
    return x * _sigmoid(x)


def _gelu(x):
    return 0.5 * x * (1.0 + jnp.tanh(0.7978845608028654 * (x + 0.044715 * (x * x * x))))


def _log_sigmoid(x):
    return jnp.minimum(x, 0.0) - jnp.log1p(jnp.exp(-jnp.abs(x)))


def _layer_norm(x, g, b):
    mu = jnp.mean(x, axis=-1, keepdims=True)
    d = x - mu
    var = jnp.mean(d * d, axis=-1, keepdims=True)
    return d * lax.rsqrt(var + LN_EPS) * g + b


def _lane_group_mask(rows, width, group, g):
    lane = lax.broadcasted_iota(jnp.int32, (rows, width), 1)
    return (lane >= g * group) & (lane < (g + 1) * group)


def _mixer_front(x, wmix_ref, bf_ref):
    z = _dot(x.astype(BF16), wmix_ref[...])
    seg = lambda i: z[:, i * W_BR:(i + 1) * W_BR]
    f = z[:, 10 * W_BR:10 * W_BR + LANES]
    lane = lax.broadcasted_iota(jnp.int32, f.shape, 1)
    logf = jnp.where(lane < HA, _log_sigmoid(f + bf_ref[...]), 0.0)
    return [seg(i) for i in range(10)], logf


def _gmlp(ub, vb, gavg_ref, lng_ref, lnb_ref, msp_ref, bsp_ref):
    rows = ub.shape[0]
    ug = _gelu(ub)
    vg = _gelu(vb)
    gavg = gavg_ref[...]
    mu = _dot3_r(vg, gavg)
    dv = vg - mu
    var = _dot3_r(dv * dv, gavg)
    vn = dv * lax.rsqrt(var + LN_EPS) * lng_ref[...] + lnb_ref[...]
    vnb = vn.astype(BF16)
    mix = bsp_ref[...]
    for g in range(GB):
        mg = _dot(msp_ref[g], vnb)
        mix = mix + jnp.where(_lane_group_mask(rows, W_BR, DB, g), mg, 0.0)
    return ug * mix, vn


def _k1_prompt_kernel(x_ref, wmix_ref, bf_ref, gavg_ref, lng_ref, lnb_ref, msp_ref, bsp_ref,
                      wc_ref, bc_ref, lcg_ref, lcb_ref, wd_ref, tril_ref,
                      q_ref, k_ref, v_ref, kb_ref, vb_ref, logf_ref, ct_ref,
                      yb_ref, yc_ref, yd_ref, sc_ref, sd_ref,
                      xpc, xpd, carry, xsh, *, tm):
    i = pl.program_id(1)

    @pl.when(i == 0)
    def _():
        xpc[0:HALO_C, :] = jnp.zeros((HALO_C, W_BR), F32)
        xpd[0:HALO_D, :] = jnp.zeros((HALO_D, W_BR), F32)
        carry[...] = jnp.zeros_like(carry)

    (q, k, v, ub, vb, ac, gc, bd, cd, hd), logf = _mixer_front(x_ref[0], wmix_ref, bf_ref)
    q_ref[0] = (q * Q_SCALE).astype(BF16)
    k_ref[0] = k
    v_ref[0] = v
    kb_ref[0] = k.astype(BF16)
    vb_ref[0] = v.astype(BF16)
    logf_ref[0] = logf[:, 0:HA]

    c = _dot3_l(tril_ref[...], logf) + carry[0:1, :]
    carry[0:1, :] = c[tm - 1:tm, :]
    ct_ref[0] = c.T[0:SUBLANES, :]

    yb, _ = _gmlp(ub, vb, gavg_ref, lng_ref, lnb_ref, msp_ref, bsp_ref)
    yb_ref[0] = yb.astype(BF16)

    cin = ac * _sigmoid(gc)
    xpc[HALO_C:HALO_C + tm, :] = cin
    span = tm + HALO_C - SUBLANES
    for r in range(1, SUBLANES):
        xsh[r - 1, 0:span, :] = xpc[r:r + span, :]
    acc = jnp.zeros((tm, W_BR), F32) + bc_ref[...]
    for j in range(KC):
        off = HALO_C - (KC - 1) + j
        r = off % SUBLANES
        a = off - r
        win = xpc[a:a + tm, :] if r == 0 else xsh[r - 1, a:a + tm, :]
        acc = acc + wc_ref[j:j + 1, :] * win
    xpc[0:HALO_C, :] = cin[tm - HALO_C:tm, :]
    sc_ref[0] = cin[tm - HALO_C:tm, :]
    yc_ref[0] = _silu(_layer_norm(acc, lcg_ref[...], lcb_ref[...])).astype(BF16)

    din = cd * hd
    xpd[HALO_D:HALO_D + tm, :] = din
    accd = jnp.zeros((tm, W_BR), F32)
    for j in range(KD):
        off = HALO_D - (KD - 1) + j
        accd = accd + wd_ref[j:j + 1, :] * xpd[off:off + tm, :]
    xpd[0:HALO_D, :] = din[tm - HALO_D:tm, :]
    sd_ref[0] = din[tm - HALO_D:tm, :]
    yd_ref[0] = (bd * accd).astype(BF16)


def _k1_prompt(l, x, wp, tm):
    b, s, _ = x.shape
    nt = s // tm
    row = lambda bb, i: (bb, i, 0)
    cst2 = lambda bb, i: (l, 0, 0)
    o256 = lambda dt: jax.ShapeDtypeStruct((b, s, W_BR), dt)
    in_specs = [
        pl.BlockSpec((1, tm, D_MODEL), row),
        _const_spec((None, D_MODEL, W_MIX), cst2),
        _const_spec((None, 1, LANES), cst2),
        _const_spec((W_BR, W_BR), lambda bb, i: (0, 0)),
        _const_spec((None, 1, W_BR), cst2),
        _const_spec((None, 1, W_BR), cst2),
        _const_spec((None, GB, tm, tm), lambda bb, i: (l, 0, 0, 0)),
        _const_spec((None, tm, W_BR), cst2),
        _const_spec((None, HALO_C, W_BR), cst2),
        _const_spec((None, 1, W_BR), cst2),
        _const_spec((None, 1, W_BR), cst2),
        _const_spec((None, 1, W_BR), cst2),
        _const_spec((None, HALO_D, W_BR), cst2),
        _const_spec((tm, tm), lambda bb, i: (0, 0)),
    ]
    out_shape = [o256(BF16), o256(F32), o256(F32), o256(BF16), o256(BF16),
                 jax.ShapeDtypeStruct((b, s, HA), F32),
                 jax.ShapeDtypeStruct((b, SUBLANES, s), F32),
                 o256(BF16), o256(BF16), o256(BF16),
                 jax.ShapeDtypeStruct((b, HALO_C, W_BR), F32),
                 jax.ShapeDtypeStruct((b, HALO_D, W_BR), F32)]
    blk = pl.BlockSpec((1, tm, W_BR), row)
    out_specs = [blk, blk, blk, blk, blk,
                 pl.BlockSpec((1, tm, HA), row),
                 pl.BlockSpec((1, SUBLANES, tm), lambda bb, i: (bb, 0, i)),
                 blk, blk, blk,
                 pl.BlockSpec((1, HALO_C, W_BR), lambda bb, i: (bb, 0, 0)),
                 pl.BlockSpec((1, HALO_D, W_BR), lambda bb, i: (bb, 0, 0))]
    return pl.pallas_call(
        functools.partial(_k1_prompt_kernel, tm=tm),
        grid=(b, nt), in_specs=in_specs, out_specs=out_specs, out_shape=out_shape,
        scratch_shapes=[pltpu.VMEM((tm + HALO_C, W_BR), F32), pltpu.VMEM((tm + HALO_D, W_BR), F32),
                        pltpu.VMEM((SUBLANES, LANES), F32),
                        pltpu.VMEM((SUBLANES - 1, tm + HALO_C - SUBLANES, W_BR), F32)],
        compiler_params=_cparams(2), name="k1_prompt",
    )(x, wp["wmix"], wp["bf"], wp["gavg"], wp["lng"], wp["lnb"], wp["msp_p"], wp["bsp_p"],
      wp["wc"], wp["bc"], wp["lcg"], wp["lcb"], wp["wd"], wp["tril_p"])


def _k1_sample_kernel(x_ref, wmix_ref, bf_ref, gavg_ref, lng_ref, lnb_ref, msp_ref, bsp_ref,
                      wc_ref, bc_ref, lcg_ref, lcb_ref, wd_ref, stc_ref, std_ref,
                      q_ref, k_ref, v_ref, logf_ref, vn_ref, yb_ref, yc_ref, yd_ref, nsc_ref, nsd_ref,
                      cin_s, ypre_s, din_s, ydpre_s, xs, xsd, *, nb, t):
    (q, k, v, ub, vb, ac, gc, bd, cd, hd), logf = _mixer_front(x_ref[...], wmix_ref, bf_ref)
    q_ref[...] = q * Q_SCALE
    k_ref[...] = k
    v_ref[...] = v
    logf_ref[...] = logf
    yb, vn = _gmlp(ub, vb, gavg_ref, lng_ref, lnb_ref, msp_ref, bsp_ref)
    yb_ref[...] = yb.astype(BF16)
    vn_ref[...] = vn

    cin_s[...] = ac * _sigmoid(gc)
    din_s[...] = cd * hd

    def body(bb, _):
        r0 = pl.multiple_of(bb * t, t)
        xs[0:KC - 1, :] = stc_ref[bb]
        xs[KC - 1:KC - 1 + t, :] = cin_s[pl.ds(r0, t), :]
        acc = jnp.zeros((t, W_BR), F32) + bc_ref[...]
        for j in range(KC):
            acc = acc + wc_ref[j:j + 1, :] * xs[j:j + t, :]
        ypre_s[pl.ds(r0, t), :] = acc
        nsc_ref[bb] = xs[t:t + KC - 1, :]

        xsd[0:KD - 1, :] = std_ref[bb]
        xsd[KD - 1:KD - 1 + t, :] = din_s[pl.ds(r0, t), :]
        accd = jnp.zeros((t, W_BR), F32)
        for j in range(KD):
            accd = accd + wd_ref[j:j + 1, :] * xsd[j:j + t, :]
        ydpre_s[pl.ds(r0, t), :] = accd
        nsd_ref[bb] = xsd[t:t + KD - 1, :]
        return 0

    lax.fori_loop(0, nb, body, 0)
    yc_ref[...] = _silu(_layer_norm(ypre_s[...], lcg_ref[...], lcb_ref[...])).astype(BF16)
    yd_ref[...] = (bd * ydpre_s[...]).astype(BF16)


def _k1_sample(l, x, stc, std, wp, nb, t):
    m = nb * t
    cst2 = lambda i: (l, 0, 0)
    cst4 = lambda i: (l, 0, 0, 0)
    full = lambda shp: pl.BlockSpec(shp, lambda i: (0,) * len(shp))
    in_specs = [
        full((m, D_MODEL)),
        pl.BlockSpec((None, D_MODEL, W_MIX), cst2),
        pl.BlockSpec((None, 1, LANES), cst2),
        full((W_BR, W_BR)),
        pl.BlockSpec((None, 1, W_BR), cst2),
        pl.BlockSpec((None, 1, W_BR), cst2),
        pl.BlockSpec((None, GB, m, m), cst4),
        pl.BlockSpec((None, m, W_BR), cst2),
        pl.BlockSpec((None, HALO_C, W_BR), cst2),
        pl.BlockSpec((None, 1, W_BR), cst2),
        pl.BlockSpec((None, 1, W_BR), cst2),
        pl.BlockSpec((None, 1, W_BR), cst2),
        pl.BlockSpec((None, HALO_D, W_BR), cst2),
        pl.BlockSpec((None, nb, KC - 1, W_BR), cst4),
        pl.BlockSpec((None, nb, KD - 1, W_BR), cst4),
    ]
    o = lambda dt: jax.ShapeDtypeStruct((m, W_BR), dt)
    out_shape = [o(F32), o(F32), o(F32), jax.ShapeDtypeStruct((m, LANES), F32), o(F32),
                 o(BF16), o(BF16), o(BF16),
                 jax.ShapeDtypeStruct((nb, KC - 1, W_BR), F32),
                 jax.ShapeDtypeStruct((nb, KD - 1, W_BR), F32)]
    out_specs = [full(s.shape) for s in out_shape]
    return pl.pallas_call(
        functools.partial(_k1_sample_kernel, nb=nb, t=t),
        grid=(1,), in_specs=in_specs, out_specs=out_specs, out_shape=out_shape,
        scratch_shapes=[pltpu.VMEM((m, W_BR), F32), pltpu.VMEM((m, W_BR), F32),
                        pltpu.VMEM((m, W_BR), F32), pltpu.VMEM((m, W_BR), F32),
                        pltpu.VMEM((KC - 1 + t + 2, W_BR), F32), pltpu.VMEM((2 * SUBLANES, W_BR), F32)],
        compiler_params=pltpu.CompilerParams(vmem_limit_bytes=VMEM_LIMIT), name="k1_sample",
    )(x, wp["wmix"], wp["bf"], wp["gavg"], wp["lng"], wp["lnb"], wp["msp_s"], wp["bsp_s"],
      wp["wc"], wp["bc"], wp["lcg"], wp["lcb"], wp["wd"], stc, std)


def _attn_prompt_kernel(qi_tab, ki_tab, q_ref, k_ref, v_ref, ck_ref, cq_ref, o_ref,
                        qm_s, m_s, acc_s, *, tq, tk):
    p = pl.program_id(1)
    qi = qi_tab[p]
    ki = ki_tab[p]
    last_k = ((qi + 1) * tq - 1) // tk

    @pl.when(ki == 0)
    def _():
        q = q_ref[0]
        for h in range(HA):
            q_pair = q[:, (h // 2) * LANES:(h // 2 + 1) * LANES]
            qm_s[h] = jnp.where(_lane_group_mask(tq, LANES, DH, h % 2), q_pair, jnp.zeros_like(q_pair))
        m_s[...] = jnp.full(m_s.shape, -jnp.inf, F32)
        acc_s[...] = jnp.zeros(acc_s.shape, F32)

    def step(masked, nk):
        k = k_ref[0, 0:nk, :]
        v = v_ref[0, 0:nk, :]
        ck = ck_ref[0, :, 0:nk]
        cq = cq_ref[0]
        if masked:
            keep = ((ki * tk - qi * tq) + lax.broadcasted_iota(jnp.int32, (tq, nk), 1)
                    <= lax.broadcasted_iota(jnp.int32, (tq, nk), 0))
        def scores(h):
            bias = (cq[h:h + 1, 0:1] - ck[h:h + 1, :]) * LOG2E
            return _dot_nt(qm_s[h], k[:, (h // 2) * LANES:(h // 2 + 1) * LANES]) + bias

        s_next = scores(0)
        for h in range(HA):
            s = s_next
            if h + 1 < HA:
                s_next = scores(h + 1)
            if masked:
                s = jnp.where(keep, s, -jnp.inf)
            m_prev = m_s[h]
            m_new = jnp.maximum(m_prev, jnp.max(s, axis=1, keepdims=True))
            alpha = jnp.exp2(m_prev - m_new)
            pr = jnp.exp2(s - m_new).astype(BF16)
            v_h = jnp.where(_lane_group_mask(nk, W_BR, DH, h), v, jnp.ones_like(v))
            acc_s[h] = acc_s[h] * alpha + _dot(pr, v_h)
            m_s[h] = m_new

    n_visible = (qi + 1) * tq - ki * tk
    half = tk // 2

    @pl.when(n_visible >= tk + tq)
    def _():
        step(False, tk)

    @pl.when((n_visible < tk + tq) & (n_visible > half))
    def _():
        step(True, tk)

    @pl.when(n_visible <= half)
    def _():
        step(True, half)

    @pl.when(ki == last_k)
    def _():
        out = jnp.zeros((tq, W_BR), F32)
        for h in range(HA):
            acc = acc_s[h]
            den_lane = ((h + 1) % HA) * DH
            out = out + jnp.where(_lane_group_mask(tq, W_BR, DH, h), acc / acc[:, den_lane:den_lane + 1], 0.0)
        o_ref[0] = out.astype(BF16)


def _attn_prompt(q, k, v, ct, tq, tk):
    b, s, _ = q.shape
    nq = s // tq
    n_k = [((i + 1) * tq - 1) // tk + 1 for i in range(nq)]
    qi_np = np.concatenate([np.full(n_k[i], i, np.int32) for i in range(nq)])
    ki_np = np.concatenate([np.arange(n_k[i], dtype=np.int32) for i in range(nq)])
    n_pairs = int(qi_np.shape[0])
    grid_spec = pltpu.PrefetchScalarGridSpec(
        num_scalar_prefetch=2, grid=(b, n_pairs),
        in_specs=[
            pl.BlockSpec((1, tq, W_BR), lambda bb, p, qt, kt: (bb, qt[p], 0)),
            pl.BlockSpec((1, tk, W_BR), lambda bb, p, qt, kt: (bb, kt[p], 0)),
            pl.BlockSpec((1, tk, W_BR), lambda bb, p, qt, kt: (bb, kt[p], 0)),
            pl.BlockSpec((1, SUBLANES, tk), lambda bb, p, qt, kt: (bb, 0, kt[p])),
            pl.BlockSpec((1, SUBLANES, tq), lambda bb, p, qt, kt: (bb, 0, qt[p])),
        ],
        out_specs=pl.BlockSpec((1, tq, W_BR), lambda bb, p, qt, kt: (bb, qt[p], 0)),
        scratch_shapes=[pltpu.VMEM((HA, tq, LANES), BF16), pltpu.VMEM((HA, tq, 1), F32),
                        pltpu.VMEM((HA, tq, W_BR), F32)],
    )
    return pl.pallas_call(
        functools.partial(_attn_prompt_kernel, tq=tq, tk=tk),
        grid_spec=grid_spec, out_shape=jax.ShapeDtypeStruct((b, s, W_BR), BF16),
        compiler_params=_cparams(2), name="attn_prompt",
    )(jnp.asarray(qi_np), jnp.asarray(ki_np), q, k, v, ct, ct)


def _page_cumsum_kernel(lf_ref, u_ref, o_ref):
    o_ref[...] = _dot3_r(lf_ref[...], u_ref[...])


def _page_cumsum(lf_rows, upper, pps):
    n_rows = lf_rows.shape[0]
    return pl.pallas_call(
        _page_cumsum_kernel, grid=(n_rows // (pps * SUBLANES),),
        in_specs=[pl.BlockSpec((pps * SUBLANES, PAGE), lambda i: (i, 0)),
                  _const_spec((PAGE, PAGE), lambda i: (0, 0))],
        out_specs=pl.BlockSpec((pps * SUBLANES, PAGE), lambda i: (i, 0)),
        out_shape=jax.ShapeDtypeStruct((n_rows, PAGE), F32),
        compiler_params=_cparams(1), name="page_cumsum",
    )(lf_rows, upper)


def _attn_sample_kernel(pt_ref, q_ref, kn_ref, vn_ref, lfn_ref, pmat_ref, a8_ref, ltri_ref, rsel_ref,
                        ck_hbm, cv_hbm, cc_hbm, o_ref, kbuf, vbuf, cbuf, sem, m_s, l_s, acc_s, carry_s,
                        *, l, nb, t, n_pages, pc):
    b = pl.program_id(0)
    c = pl.program_id(1)
    nc = n_pages // pc
    step_id = b * nc + c
    slot = step_id % 2
    rows = HA * t
    ng = rows // SUBLANES

    def copies(bb, cc, sl, p):
        pg = pt_ref[bb, cc * pc + p]
        return (pltpu.make_async_copy(ck_hbm.at[l, pg], kbuf.at[sl, p], sem.at[sl, 0]),
                pltpu.make_async_copy(cv_hbm.at[l, pg], vbuf.at[sl, p], sem.at[sl, 1]),
                pltpu.make_async_copy(cc_hbm.at[l, pg], cbuf.at[sl, p], sem.at[sl, 2]))

    def issue(bb, cc, sl):
        def body(p, _):
            for cp in copies(bb, cc, sl, p):
                cp.start()
            return 0
        lax.fori_loop(0, pc, body, 0)

    @pl.when(step_id == 0)
    def _():
        issue(0, 0, 0)

    @pl.when(step_id + 1 < nb * nc)
    def _():
        nxt = step_id + 1
        issue(nxt // nc, nxt % nc, 1 - slot)

    def wait_body(p, _):
        for cp in copies(b, c, slot, p):
            cp.wait()
        return 0
    lax.fori_loop(0, pc, wait_body, 0)

    @pl.when(c == 0)
    def _():
        m_s[...] = jnp.full(m_s.shape, -jnp.inf, F32)
        l_s[...] = jnp.zeros(l_s.shape, F32)
        acc_s[...] = jnp.zeros(acc_s.shape, F32)
        carry_s[...] = jnp.zeros(carry_s.shape, F32)

    q = q_ref[0]
    sub = lax.broadcasted_iota(jnp.int32, (SUBLANES, W_BR), 0)
    lane = lax.broadcasted_iota(jnp.int32, (SUBLANES, W_BR), 1)
    head_ok = (lane // DH) == (sub % HA)
    qm = []
    for g in range(t // 2):
        qa = jnp.broadcast_to(q[2 * g:2 * g + 1, :], (SUBLANES, W_BR))
        qb = jnp.broadcast_to(q[2 * g + 1:2 * g + 2, :], (SUBLANES, W_BR))
        qm.append(jnp.where(head_ok, jnp.where(sub < HA, qa, qb), 0.0))
    qm = jnp.concatenate(qm, axis=0).astype(BF16)

    def online_update(s3, pv_fn):
        m_prev = m_s[...]
        m_new = jnp.maximum(m_prev, jnp.max(s3, axis=2, keepdims=True))
        alpha = jnp.exp2(m_prev - m_new)
        pr = jnp.exp2(s3 - m_new)
        l_s[...] = alpha * l_s[...] + jnp.sum(pr, axis=2, keepdims=True)
        acc_s[...] = acc_s[...] * alpha.reshape(rows, 1) + pv_fn(pr.reshape(rows, s3.shape[2]).astype(BF16))
        m_s[...] = m_new

    c2 = cbuf[slot].reshape(pc * SUBLANES, LANES)
    off = _dot3_l(pmat_ref[...], c2)
    carry = jnp.concatenate([carry_s[:, 0:1]] * pc, axis=0)
    cf = c2 + off[:, LANES - 1:LANES] + carry
    tot = cf[(pc - 1) * SUBLANES:pc * SUBLANES, LANES - 1:LANES]
    carry_s[...] = jnp.broadcast_to(tot, carry_s.shape)
    c_past = jnp.concatenate([cf[j * SUBLANES:(j + 1) * SUBLANES, :] for j in range(pc)], axis=1)

    s_past = jnp.concatenate([_dot(qm, kbuf[slot, j].astype(BF16)) for j in range(pc)], axis=1)
    s_past = s_past.reshape(ng, SUBLANES, pc * PAGE) - (c_past * LOG2E)[None]

    def pv_past(p2):
        acc = jnp.zeros((rows, W_BR), F32)
        for j in range(pc):
            acc = acc + _dot_nt(p2[:, j * PAGE:(j + 1) * PAGE], vbuf[slot, j].astype(BF16))
        return acc

    online_update(s_past, pv_past)

    @pl.when(c == nc - 1)
    def _():
        lfp = jnp.concatenate([lfn_ref[0], jnp.zeros((LANES - t, LANES), F32)], axis=0)
        cum_new = _dot3_l(ltri_ref[...], lfp)
        c_new = _dot3_nt(a8_ref[...], cum_new) + tot
        zpad = jnp.zeros((LANES - t, W_BR), F32)
        kn = jnp.concatenate([kn_ref[0], zpad], axis=0).astype(BF16)
        vn = jnp.concatenate([vn_ref[0], zpad], axis=0).astype(BF16)
        s_new = _dot_nt(qm, kn).reshape(ng, SUBLANES, LANES) - (c_new * LOG2E)[None]
        gi = lax.broadcasted_iota(jnp.int32, s_new.shape, 0)
        si = lax.broadcasted_iota(jnp.int32, s_new.shape, 1)
        ji = lax.broadcasted_iota(jnp.int32, s_new.shape, 2)
        s_new = jnp.where(ji <= 2 * gi + si // HA, s_new, -jnp.inf)
        online_update(s_new, lambda p2: _dot(p2, vn))

        o = acc_s[...] / l_s[...].reshape(rows, 1)
        sub_r = lax.broadcasted_iota(jnp.int32, (rows, W_BR), 0)
        lane_r = lax.broadcasted_iota(jnp.int32, (rows, W_BR), 1)
        om = jnp.where((lane_r // DH) == (sub_r % HA), o, 0.0)
        o_ref[0] = _dot(rsel_ref[...], om.astype(BF16))


def _attn_sample(l, page_table, q, kn, vn, lfn, wp, ck, cv, cc, pc):
    nb, t, _ = q.shape
    n_pages = page_table.shape[1]
    rows = HA * t
    cst = lambda shp: pl.BlockSpec(shp, lambda b, c, pt: (0,) * len(shp))
    per_b = lambda w: pl.BlockSpec((1, t, w), lambda b, c, pt: (b, 0, 0))
    grid_spec = pltpu.PrefetchScalarGridSpec(
        num_scalar_prefetch=1, grid=(nb, n_pages // pc),
        in_specs=[per_b(W_BR), per_b(W_BR), per_b(W_BR), per_b(LANES),
                  cst((pc * SUBLANES, pc * SUBLANES)), cst((SUBLANES, LANES)),
                  cst((LANES, LANES)), cst((t, rows)),
                  pl.BlockSpec(memory_space=pl.ANY), pl.BlockSpec(memory_space=pl.ANY),
                  pl.BlockSpec(memory_space=pl.ANY)],
        out_specs=per_b(W_BR),
        scratch_shapes=[pltpu.VMEM((2, pc, W_BR, PAGE), F32),
                        pltpu.VMEM((2, pc, W_BR, PAGE), F32),
                        pltpu.VMEM((2, pc, SUBLANES, LANES), F32),
                        pltpu.SemaphoreType.DMA((2, 3)),
                        pltpu.VMEM((rows // SUBLANES, SUBLANES, 1), F32),
                        pltpu.VMEM((rows // SUBLANES, SUBLANES, 1), F32),
                        pltpu.VMEM((rows, W_BR), F32),
                        pltpu.VMEM((SUBLANES, LANES), F32)],
    )
    return pl.pallas_call(
        functools.partial(_attn_sample_kernel, l=l, nb=nb, t=t, n_pages=n_pages, pc=pc),
        grid_spec=grid_spec, out_shape=jax.ShapeDtypeStruct((nb, t, W_BR), F32),
        compiler_params=_cparams(2), name="attn_sample",
    )(page_table, q, kn, vn, lfn, wp["pmat"], wp["a8"], wp["ltri"], wp["rsel"], ck, cv, cc)


def _merge_kernel(x_ref, ya_ref, yb_ref, yc_ref, yd_ref, wg_ref, wbr_ref, wo_ref, g_ref, b_ref, o_ref,
                  *, alpha):
    x = x_ref[...]
    xb = x.astype(BF16)
    ys = (ya_ref, yb_ref, yc_ref, yd_ref)
    acc = jnp.zeros(x.shape, F32)
    for n in range(N_BRANCH):
        gate = _sigmoid(_dot(xb, wg_ref[:, n * D_MODEL:(n + 1) * D_MODEL]))
        acc = acc + gate * _dot(ys[n][...].astype(BF16), wbr_ref[n])
    mix = _dot(acc.astype(BF16), wo_ref[...])
    o_ref[...] = _layer_norm(alpha * x + mix, g_ref[...], b_ref[...])


def _merge(l, x, ya, yb, yc, yd, wp, tm, alpha):
    m = x.shape[0]
    row = lambda w: pl.BlockSpec((tm, w), lambda i: (i, 0))
    c3 = lambda i: (l, 0, 0)
    return pl.pallas_call(
        functools.partial(_merge_kernel, alpha=alpha),
        grid=(m // tm,),
        in_specs=[row(D_MODEL), row(W_BR), row(W_BR), row(W_BR), row(W_BR),
                  _const_spec((None, D_MODEL, N_BRANCH * D_MODEL), c3),
                  _const_spec((None, N_BRANCH, W_BR, D_MODEL), lambda i: (l, 0, 0, 0)),
                  _const_spec((None, D_MODEL, D_MODEL), c3),
                  _const_spec((None, 1, D_MODEL), c3), _const_spec((None, 1, D_MODEL), c3)],
        out_specs=row(D_MODEL), out_shape=jax.ShapeDtypeStruct((m, D_MODEL), F32),
        compiler_params=_cparams(1), name="merge",
    )(x, ya, yb, yc, yd, wp["wgate"], wp["wbr"], wp["wout"], wp["ln1g"], wp["ln1b"])


def _route_tokens_on_lanes(lt):
    sub = lax.broadcasted_iota(jnp.int32, lt.shape, 0)
    big = jnp.int32(LANES)
    lg = jnp.where(sub < N_GROUPS, lt, -jnp.inf)
    mg = jnp.max(lg, axis=0, keepdims=True)
    p_top = 1.0 / jnp.sum(jnp.exp(lg - mg), axis=0, keepdims=True)
    g_idx = jnp.min(jnp.where(lg == mg, sub, big), axis=0, keepdims=True)
    lo = N_GROUPS + g_idx * EXP_PER_GROUP
    le = jnp.where((sub >= lo) & (sub < lo + EXP_PER_GROUP), lt, -jnp.inf)
    v1 = jnp.max(le, axis=0, keepdims=True)
    i1 = jnp.min(jnp.where(le == v1, sub, big), axis=0, keepdims=True)
    le2 = jnp.where(sub == i1, -jnp.inf, le)
    v2 = jnp.max(le2, axis=0, keepdims=True)
    i2 = jnp.min(jnp.where(le2 == v2, sub, big), axis=0, keepdims=True)
    e2 = jnp.exp(v2 - v1)
    w1 = p_top / (1.0 + e2)
    w2 = p_top * e2 / (1.0 + e2)
    comb_t = jnp.where(sub == i1, w1, 0.0) + jnp.where(sub == i2, w2, 0.0)
    return g_idx, comb_t


def _moe_kernel(x_ref, ple_ref, wrh_ref, wrl_ref, br_ref, triu_ref, wg_ref, wu_ref, wd_ref, wpg_ref, wpp_ref,
                g_ref, b_ref, o_ref, moe_s, *, alpha, cap):
    x = x_ref[...]
    tm = x.shape[0]
    xb = x.astype(BF16)
    xm = (x - xb.astype(F32)).astype(BF16)
    nr = 3 * SUBLANES
    kpad = -(-cap // LANES) * LANES

    wrh = wrh_ref[...]
    lt = (_dot_nt(wrh, xb) + _dot_nt(wrl_ref[...], xb) + _dot_nt(wrh, xm) + br_ref[...])[0:nr]
    g_idx, comb_t = _route_tokens_on_lanes(lt)

    sub16 = lax.broadcasted_iota(jnp.int32, (2 * SUBLANES, tm), 0)
    onehot = jnp.where(sub16 == g_idx, 1.0, 0.0)
    before = _dot(onehot.astype(BF16), triu_ref[...])
    rank = jnp.sum(onehot * before, axis=0, keepdims=True)
    n_max = jnp.max(before[:, tm - 1:tm] + onehot[:, tm - 1:tm])

    sub8 = lax.broadcasted_iota(jnp.int32, (SUBLANES, tm), 0)
    extra = jnp.where(sub8 == 0, rank, jnp.where(sub8 == 1, g_idx.astype(F32), 0.0))
    col = jnp.concatenate([comb_t, extra, jnp.zeros((LANES - nr - SUBLANES, tm), F32)], axis=0).T
    rank_c = col[:, nr:nr + 1].astype(jnp.int32)
    g_c = col[:, nr + 1:nr + 2].astype(jnp.int32)

    def experts(rows_b, comb_rows, e_list):
        y = jnp.zeros((rows_b.shape[0], D_MODEL), F32)
        e_list = list(e_list)
        up = lambda e: (_dot(rows_b, wg_ref[e]), _dot(rows_b, wu_ref[e]))
        nxt = up(e_list[0])
        for i, e in enumerate(e_list):
            hg, hu = nxt
            if i + 1 < len(e_list):
                nxt = up(e_list[i + 1])
            hh = _silu(hg) * hu * comb_rows[:, N_GROUPS + e:N_GROUPS + e + 1]
            y = y + _dot(hh.astype(BF16), wd_ref[e])
        return y

    @pl.when(n_max <= cap)
    def _():
        comb_hi = col.astype(BF16)
        comb_lo = (col - comb_hi.astype(F32)).astype(BF16)
        r_iota = lax.broadcasted_iota(jnp.int32, (cap, tm), 0)
        c_iota = lax.broadcasted_iota(jnp.int32, (tm, kpad), 1)
        rank_i = rank.astype(jnp.int32)
        acc = jnp.zeros((tm, D_MODEL), F32)
        for g in range(N_GROUPS):
            pg = jnp.where((r_iota == rank_i) & (g_idx == g), 1.0, 0.0).astype(BF16)
            pgt = jnp.where((c_iota == rank_c) & (g_c == g), 1.0, 0.0).astype(BF16)
            xs = _dot(pg, xb).astype(BF16)
            cs = _dot(pg, comb_hi) + _dot(pg, comb_lo)
            y = experts(xs, cs, range(g * EXP_PER_GROUP, (g + 1) * EXP_PER_GROUP))
            if kpad > cap:
                y = jnp.concatenate([y, jnp.zeros((kpad - cap, D_MODEL), F32)], axis=0)
            acc = acc + _dot(pgt, y.astype(BF16))
        moe_s[...] = acc

    @pl.when(n_max > cap)
    def _():
        moe_s[...] = experts(xb, col, range(N_EXPERTS))

    ple_term = _sigmoid(_dot(xb, wpg_ref[...])) * _dot(ple_ref[...].astype(BF16), wpp_ref[...])
    o_ref[...] = _layer_norm(alpha * x + moe_s[...] + ple_term, g_ref[...], b_ref[...])


def _moe(l, x, ple, wp, tm, alpha, triu):
    m = x.shape[0]
    cap = (tm * 5) // 16
    c3 = lambda i: (l, 0, 0)
    c4 = lambda i: (l, 0, 0, 0)
    return pl.pallas_call(
        functools.partial(_moe_kernel, alpha=alpha, cap=cap),
        grid=(m // tm,),
        in_specs=[pl.BlockSpec((tm, D_MODEL), lambda i: (i, 0)),
                  pl.BlockSpec((None, tm, D_PLE), lambda i: (l, i, 0)),
                  _const_spec((None, LANES, D_MODEL), c3), _const_spec((None, LANES, D_MODEL), c3),
                  _const_spec((None, LANES, 1), c3),
                  _const_spec((tm, tm), lambda i: (0, 0)),
                  _const_spec((None, N_EXPERTS, D_MODEL, D_EXPERT), c4),
                  _const_spec((None, N_EXPERTS, D_MODEL, D_EXPERT), c4),
                  _const_spec((None, N_EXPERTS, D_EXPERT, D_MODEL), c4),
                  _const_spec((None, D_MODEL, D_MODEL), c3), _const_spec((None, D_PLE, D_MODEL), c3),
                  _const_spec((None, 1, D_MODEL), c3), _const_spec((None, 1, D_MODEL), c3)],
        out_specs=pl.BlockSpec((tm, D_MODEL), lambda i: (i, 0)),
        out_shape=jax.ShapeDtypeStruct((m, D_MODEL), F32),
        scratch_shapes=[pltpu.VMEM((tm, D_MODEL), F32)],
        compiler_params=_cparams(1), name="moe",
    )(x, ple, wp["wr_hi"], wp["wr_lo"], wp["br"], triu, wp["weg"], wp["weu"], wp["wed"],
      wp["wpg"], wp["wpp"], wp["ln2g"], wp["ln2b"])


def _kron_tril(w_spatial, n_blocks, n):
    w = jnp.tril(w_spatial)[:, :, :n, :n]
    eye = jnp.eye(n_blocks, dtype=w.dtype)
    out = jnp.einsum("ab,lgij->lgaibj", eye, w)
    d = w_spatial.shape[0]
    return out.reshape(d, GB, n_blocks * n, n_blocks * n).astype(BF16)


def _prepare(w_in, b_forget, ln_gmlp_g, ln_gmlp_b, w_spatial, b_spatial, w_conv_c, b_conv_c, ln_conv_g,
             ln_conv_b, w_conv_d, w_branch, w_out, ln1_g, ln1_b, w_router_group, b_router_group,
             w_router_expert, b_router_expert, w_exp_gate, w_exp_up, w_exp_down, w_ple_gate, w_ple_proj,
             ln2_g, ln2_b, tm_p, nb, t, pc, pps):
    d = w_in.shape[0]
    o_f = 3 * W_BR
    o_rest = o_f + HA
    o_gate = o_rest + 7 * W_BR
    r3 = lambda a: a.reshape(d, 1, -1)
    wp = {}
    wp["wmix"] = jnp.concatenate(
        [w_in[:, :, 0:o_f], w_in[:, :, o_rest:o_gate],
         jnp.pad(w_in[:, :, o_f:o_rest], ((0, 0), (0, 0), (0, LANES - HA)))], axis=-1).astype(BF16)
    wp["wgate"] = w_in[:, :, o_gate:].astype(BF16)
    wp["bf"] = jnp.pad(b_forget, ((0, 0), (0, LANES - HA))).reshape(d, 1, LANES)
    gi = np.arange(W_BR) // DB
    wp["gavg"] = jnp.asarray((gi[:, None] == gi[None, :]).astype(np.float32) / DB, dtype=BF16)
    wp["lng"] = r3(ln_gmlp_g)
    wp["lnb"] = r3(ln_gmlp_b)
    wp["msp_p"] = _kron_tril(w_spatial, tm_p // CHUNK, CHUNK)
    wp["msp_s"] = _kron_tril(w_spatial, nb, t)
    bsp = jnp.repeat(jnp.swapaxes(b_spatial, 1, 2), DB, axis=2)
    wp["bsp_p"] = jnp.tile(bsp, (1, tm_p // CHUNK, 1))
    wp["bsp_s"] = jnp.tile(bsp[:, :t, :], (1, nb, 1))
    wp["wc"] = jnp.pad(w_conv_c, ((0, 0), (0, HALO_C - KC), (0, 0)))
    wp["bc"] = r3(b_conv_c)
    wp["lcg"] = r3(ln_conv_g)
    wp["lcb"] = r3(ln_conv_b)
    wp["wd"] = jnp.pad(w_conv_d, ((0, 0), (0, HALO_D - KD), (0, 0)))
    wp["tril_p"] = jnp.asarray(np.tril(np.ones((tm_p, tm_p), np.float32)), dtype=BF16)
    wp["wbr"] = w_branch.astype(BF16)
    wp["wout"] = w_out.astype(BF16)
    wp["ln1g"] = r3(ln1_g)
    wp["ln1b"] = r3(ln1_b)
    wr = jnp.swapaxes(jnp.concatenate([w_router_group, w_router_expert,
                                       jnp.zeros((d, D_MODEL, LANES - N_GROUPS - N_EXPERTS), F32)], axis=-1), 1, 2)
    wr_hi = wr.astype(BF16)
    wp["wr_hi"] = wr_hi
    wp["wr_lo"] = (wr - wr_hi.astype(F32)).astype(BF16)
    wp["br"] = jnp.concatenate([b_router_group, b_router_expert,
                                jnp.zeros((d, LANES - N_GROUPS - N_EXPERTS), F32)], axis=-1).reshape(d, LANES, 1)
    strict_upper = lambda n: jnp.asarray(np.triu(np.ones((n, n), np.float32), 1), dtype=BF16)
    wp["triu_p"] = strict_upper(tm_p)
    wp["triu_s"] = strict_upper(nb * t)
    wp["weg"] = w_exp_gate.astype(BF16)
    wp["weu"] = w_exp_up.astype(BF16)
    wp["wed"] = w_exp_down.astype(BF16)
    wp["wpg"] = w_ple_gate.astype(BF16)
    wp["wpp"] = w_ple_proj.astype(BF16)
    wp["ln2g"] = r3(ln2_g)
    wp["ln2b"] = r3(ln2_b)

    rows = pc * SUBLANES
    ri = np.arange(rows)
    pm = ((ri[:, None] % SUBLANES) == (ri[None, :] % SUBLANES)) & ((ri[None, :] // SUBLANES) < (ri[:, None] // SUBLANES))
    wp["pmat"] = jnp.asarray(pm.astype(np.float32), dtype=BF16)
    a8 = (np.arange(LANES)[None, :] == (np.arange(SUBLANES)[:, None] % HA)).astype(np.float32)
    wp["a8"] = jnp.asarray(a8, dtype=BF16)
    wp["ltri"] = jnp.asarray(np.tril(np.ones((LANES, LANES), np.float32)), dtype=BF16)
    rsel = ((np.arange(HA * t)[None, :] // HA) == np.arange(t)[:, None]).astype(np.float32)
    wp["rsel"] = jnp.asarray(rsel, dtype=BF16)
    wp["upper"] = jnp.asarray(np.triu(np.ones((PAGE, PAGE), np.float32)), dtype=BF16)
    return wp


def kernel(x_prompt, x_sample, cache_k, cache_v, cache_logf, state_conv_c, state_conv_d, page_table,
           p_prompt, p_sample, w_in, b_forget, ln_gmlp_g, ln_gmlp_b, w_spatial, b_spatial, w_conv_c,
           b_conv_c, ln_conv_g, ln_conv_b, w_conv_d, w_branch, w_out, ln1_g, ln1_b, w_router_group,
           b_router_group, w_router_expert, b_router_expert, w_exp_gate, w_exp_up, w_exp_down,
           w_ple_gate, w_ple_proj, ln2_g, ln2_b):
    depth = w_in.shape[0]
    alpha = float((2 * depth) ** 0.25)
    b, s, _ = x_prompt.shape
    nb, t, _ = x_sample.shape
    n_pool = cache_k.shape[1]
    n_pages = page_table.shape[1]
    tm_p = min(512, s)
    tq, tk = min(512, s), min(1024, s)
    ms = nb * t
    pc = min(64, n_pages)
    n_pg = depth * n_pool
    pps = next(c for c in (256, 128, 64, 32, 16, 8, 4, 2, 1) if n_pg % c == 0)

    wp = _prepare(w_in, b_forget, ln_gmlp_g, ln_gmlp_b, w_spatial, b_spatial, w_conv_c, b_conv_c,
                  ln_conv_g, ln_conv_b, w_conv_d, w_branch, w_out, ln1_g, ln1_b, w_router_group,
                  b_router_group, w_router_expert, b_router_expert, w_exp_gate, w_exp_up, w_exp_down,
                  w_ple_gate, w_ple_proj, ln2_g, ln2_b, tm_p, nb, t, pc, pps)

    ck = jnp.transpose(cache_k, (0, 1, 3, 4, 2)).reshape(depth, n_pool, W_BR, PAGE)
    cv = jnp.transpose(cache_v, (0, 1, 3, 4, 2)).reshape(depth, n_pool, W_BR, PAGE)
    lf_t = jnp.transpose(cache_logf, (0, 1, 3, 2))
    lf_rows = jnp.concatenate([lf_t, lf_t], axis=2).reshape(n_pg * SUBLANES, PAGE)
    cc = _page_cumsum(lf_rows, wp["upper"], pps).reshape(depth, n_pool, SUBLANES, LANES)

    pp = p_prompt.reshape(depth, b * s, D_PLE)
    ps = p_sample.reshape(depth, ms, D_PLE)

    xp = x_prompt
    xs = x_sample.reshape(ms, D_MODEL)
    kp, vp, lfp, ccp, cdp = [], [], [], [], []
    ks_, vs_, lfs, ccs, cds, gvs = [], [], [], [], [], []
    for l in range(depth):
        (q, k32, v32, kb, vb, logf, ct, yb, yc, yd, stc, std) = _k1_prompt(l, xp, wp, tm_p)
        ya = _attn_prompt(q, kb, vb, ct, tq, tk)
        f2 = lambda a: a.reshape(b * s, a.shape[-1])
        x1 = _merge(l, f2(xp), f2(ya), f2(yb), f2(yc), f2(yd), wp, tm_p, alpha)
        xp = _moe(l, x1, pp, wp, tm_p, alpha, wp["triu_p"]).reshape(b, s, D_MODEL)
        kp.append(k32.reshape(b, s, HA, DH))
        vp.append(v32.reshape(b, s, HA, DH))
        lfp.append(logf)
        ccp.append(stc[:, HALO_C - (KC - 1):, :])
        cdp.append(std[:, HALO_D - (KD - 1):, :])

        (qs, k_s, v_s, lf_s, vn_s, yb_s, yc_s, yd_s, nsc, nsd) = _k1_sample(
            l, xs, state_conv_c, state_conv_d, wp, nb, t)
        r3 = lambda a: a.reshape(nb, t, a.shape[-1])
        ya_s = _attn_sample(l, page_table, r3(qs), r3(k_s), r3(v_s), r3(lf_s), wp, ck, cv, cc, pc)
        x1s = _merge(l, xs, ya_s.reshape(ms, W_BR), yb_s, yc_s, yd_s, wp, ms, alpha)
        xs = _moe(l, x1s, ps, wp, ms, alpha, wp["triu_s"])
        ks_.append(k_s.reshape(nb, t, HA, DH))
        vs_.append(v_s.reshape(nb, t, HA, DH))
        lfs.append(lf_s[:, 0:HA].reshape(nb, t, HA))
        ccs.append(nsc)
        cds.append(nsd)
        gvs.append(vn_s.reshape(nb, t, GB, DB))

    return (xp, xs.reshape(nb, t, D_MODEL), jnp.stack(kp), jnp.stack(vp), jnp.stack(lfp),
            jnp.stack(ks_), jnp.stack(vs_), jnp.stack(lfs), jnp.stack(ccp), jnp.stack(ccs),
            jnp.stack(cdp), jnp.stack(cds), jnp.stack(gvs))
```

```python
import functools

import numpy as np
import jax
import jax.numpy as jnp
from jax import lax
from jax.experimental import pallas as pl
from jax.experimental.pallas import tpu as pltpu

F32 = jnp.float32
BF16 = jnp.bfloat16

D_MODEL = 1024
W_BR = 256
HA = 4
DH = 64
GB = 4
DB = 64
CHUNK = 128
KC = 31
KD = 3
N_BRANCH = 4
N_GROUPS = 4
EXP_PER_GROUP = 4
N_EXPERTS = 16
D_EXPERT = 256
D_PLE = 256
PAGE = 128
LN_EPS = 1e-5
ATTN_SCALE = DH ** -0.5
LOG2E = 1.4426950408889634
Q_SCALE = ATTN_SCALE * LOG2E
LANES = 128
SUBLANES = 8
HALO_C = 32
HALO_D = 8
W_MIX = 10 * W_BR + LANES
VMEM_LIMIT = 56 * 1024 * 1024


def _cparams(n_axes):
    return pltpu.CompilerParams(dimension_semantics=("arbitrary",) * n_axes,
                                vmem_limit_bytes=VMEM_LIMIT)


def _const_spec(shape, index_map):
    return pl.BlockSpec(shape, index_map, pipeline_mode=pl.Buffered(1))


def _dot(a, b):
    return jnp.dot(a, b, preferred_element_type=F32)


def _dot_nt(a, b):
    return lax.dot_general(a, b, (((1,), (1,)), ((), ())), preferred_element_type=F32)


def _split3(x):
    hi = x.astype(BF16)
    r = x - hi.astype(F32)
    mid = r.astype(BF16)
    lo = (r - mid.astype(F32)).astype(BF16)
    return hi, mid, lo


def _dot3_r(x, m):
    hi, mid, lo = _split3(x)
    return _dot(hi, m) + _dot(mid, m) + _dot(lo, m)


def _dot3_l(m, x):
    hi, mid, lo = _split3(x)
    return _dot(m, hi) + _dot(m, mid) + _dot(m, lo)


def _dot3_nt(m, x):
    hi, mid, lo = _split3(x)
    return _dot_nt(m, hi) + _dot_nt(m, mid) + _dot_nt(m, lo)


def _sigmoid(x):
    return 0.5 * jnp.tanh(0.5 * x) + 0.5


def _silu(x):
    return x * _sigmoid(x)


def _gelu(x):
    return 0.5 * x * (1.0 + jnp.tanh(0.7978845608028654 * (x + 0.044715 * (x * x * x))))


def _log_sigmoid(x):
    return jnp.minimum(x, 0.0) - jnp.log1p(jnp.exp(-jnp.abs(x)))


def _layer_norm(x, g, b):
    mu = jnp.mean(x, axis=-1, keepdims=True)
    d = x - mu
    var = jnp.mean(d * d, axis=-1, keepdims=True)
    return d * lax.rsqrt(var + LN_EPS) * g + b


def _lane_group_mask(rows, width, group, g):
    lane = lax.broadcasted_iota(jnp.int32, (rows, width), 1)
    return (lane >= g * group) & (lane < (g + 1) * group)


def _mixer_front(x, wmix_ref, bf_ref):
    z = _dot(x.astype(BF16), wmix_ref[...])
    seg = lambda i: z[:, i * W_BR:(i + 1) * W_BR]
    f = z[:, 10 * W_BR:10 * W_BR + LANES]
    lane = lax.broadcasted_iota(jnp.int32, f.shape, 1)
    logf = jnp.where(lane < HA, _log_sigmoid(f + bf_ref[...]), 0.0)
    return [seg(i) for i in range(10)], logf


def _gmlp(ub, vb, gavg_ref, lng_ref, lnb_ref, msp_ref, bsp_ref):
    rows = ub.shape[0]
    ug = _gelu(ub)
    vg = _gelu(vb)
    gavg = gavg_ref[...]
    mu = _dot3_r(vg, gavg)
    dv = vg - mu
    var = _dot3_r(dv * dv, gavg)
    vn = dv * lax.rsqrt(var + LN_EPS) * lng_ref[...] + lnb_ref[...]
    vnb = vn.astype(BF16)
    mix = bsp_ref[...]
    for g in range(GB):
        mg = _dot(msp_ref[g], vnb)
        mix = mix + jnp.where(_lane_group_mask(rows, W_BR, DB, g), mg, 0.0)
    return ug * mix, vn


def _k1_prompt_kernel(x_ref, wmix_ref, bf_ref, gavg_ref, lng_ref, lnb_ref, msp_ref, bsp_ref,
                      wc_ref, bc_ref, lcg_ref, lcb_ref, wd_ref, tril_ref,
                      q_ref, k_ref, v_ref, kb_ref, vb_ref, logf_ref, ct_ref,
                      yb_ref, yc_ref, yd_ref, sc_ref, sd_ref,
                      xpc, xpd, carry, xsh, *, tm):
    i = pl.program_id(1)

    @pl.when(i == 0)
    def _():
        xpc[0:HALO_C, :] = jnp.zeros((HALO_C, W_BR), F32)
        xpd[0:HALO_D, :] = jnp.zeros((HALO_D, W_BR), F32)
        carry[...] = jnp.zeros_like(carry)

    (q, k, v, ub, vb, ac, gc, bd, cd, hd), logf = _mixer_front(x_ref[0], wmix_ref, bf_ref)
    q_ref[0] = (q * Q_SCALE).astype(BF16)
    k_ref[0] = k
    v_ref[0] = v
    kb_ref[0] = k.astype(BF16)
    vb_ref[0] = v.astype(BF16)
    logf_ref[0] = logf[:, 0:HA]

    c = _dot3_l(tril_ref[...], logf) + carry[0:1, :]
    carry[0:1, :] = c[tm - 1:tm, :]
    ct_ref[0] = c.T[0:SUBLANES, :]

    yb, _ = _gmlp(ub, vb, gavg_ref, lng_ref, lnb_ref, msp_ref, bsp_ref)
    yb_ref[0] = yb.astype(BF16)

    cin = ac * _sigmoid(gc)
    xpc[HALO_C:HALO_C + tm, :] = cin
    span = tm + HALO_C - SUBLANES
    for r in range(1, SUBLANES):
        xsh[r - 1, 0:span, :] = xpc[r:r + span, :]
    acc = jnp.zeros((tm, W_BR), F32) + bc_ref[...]
    for j in range(KC):
        off = HALO_C - (KC - 1) + j
        r = off % SUBLANES
        a = off - r
        win = xpc[a:a + tm, :] if r == 0 else xsh[r - 1, a:a + tm, :]
        acc = acc + wc_ref[j:j + 1, :] * win
    xpc[0:HALO_C, :] = cin[tm - HALO_C:tm, :]
    sc_ref[0] = cin[tm - HALO_C:tm, :]
    yc_ref[0] = _silu(_layer_norm(acc, lcg_ref[...], lcb_ref[...])).astype(BF16)

    din = cd * hd
    xpd[HALO_D:HALO_D + tm, :] = din
    accd = jnp.zeros((tm, W_BR), F32)
    for j in range(KD):
        off = HALO_D - (KD - 1) + j
        accd = accd + wd_ref[j:j + 1, :] * xpd[off:off + tm, :]
    xpd[0:HALO_D, :] = din[tm - HALO_D:tm, :]
    sd_ref[0] = din[tm - HALO_D:tm, :]
    yd_ref[0] = (bd * accd).astype(BF16)


def _k1_prompt(l, x, wp, tm):
    b, s, _ = x.shape
    nt = s // tm
    row = lambda bb, i: (bb, i, 0)
    cst2 = lambda bb, i: (l, 0, 0)
    o256 = lambda dt: jax.ShapeDtypeStruct((b, s, W_BR), dt)
    in_specs = [
        pl.BlockSpec((1, tm, D_MODEL), row),
        _const_spec((None, D_MODEL, W_MIX), cst2),
        _const_spec((None, 1, LANES), cst2),
        _const_spec((W_BR, W_BR), lambda bb, i: (0, 0)),
        _const_spec((None, 1, W_BR), cst2),
        _const_spec((None, 1, W_BR), cst2),
        _const_spec((None, GB, tm, tm), lambda bb, i: (l, 0, 0, 0)),
        _const_spec((None, tm, W_BR), cst2),
        _const_spec((None, HALO_C, W_BR), cst2),
        _const_spec((None, 1, W_BR), cst2),
        _const_spec((None, 1, W_BR), cst2),
        _const_spec((None, 1, W_BR), cst2),
        _const_spec((None, HALO_D, W_BR), cst2),
        _const_spec((tm, tm), lambda bb, i: (0, 0)),
    ]
    out_shape = [o256(BF16), o256(F32), o256(F32), o256(BF16), o256(BF16),
                 jax.ShapeDtypeStruct((b, s, HA), F32),
                 jax.ShapeDtypeStruct((b, SUBLANES, s), F32),
                 o256(BF16), o256(BF16), o256(BF16),
                 jax.ShapeDtypeStruct((b, HALO_C, W_BR), F32),
                 jax.ShapeDtypeStruct((b, HALO_D, W_BR), F32)]
    blk = pl.BlockSpec((1, tm, W_BR), row)
    out_specs = [blk, blk, blk, blk, blk,
                 pl.BlockSpec((1, tm, HA), row),
                 pl.BlockSpec((1, SUBLANES, tm), lambda bb, i: (bb, 0, i)),
                 blk, blk, blk,
                 pl.BlockSpec((1, HALO_C, W_BR), lambda bb, i: (bb, 0, 0)),
                 pl.BlockSpec((1, HALO_D, W_BR), lambda bb, i: (bb, 0, 0))]
    return pl.pallas_call(
        functools.partial(_k1_prompt_kernel, tm=tm),
        grid=(b, nt), in_specs=in_specs, out_specs=out_specs, out_shape=out_shape,
        scratch_shapes=[pltpu.VMEM((tm + HALO_C, W_BR), F32), pltpu.VMEM((tm + HALO_D, W_BR), F32),
                        pltpu.VMEM((SUBLANES, LANES), F32),
                        pltpu.VMEM((SUBLANES - 1, tm + HALO_C - SUBLANES, W_BR), F32)],
        compiler_params=_cparams(2), name="k1_prompt",
    )(x, wp["wmix"], wp["bf"], wp["gavg"], wp["lng"], wp["lnb"], wp["msp_p"], wp["bsp_p"],
      wp["wc"], wp["bc"], wp["lcg"], wp["lcb"], wp["wd"], wp["tril_p"])


def _k1_sample_kernel(x_ref, wmix_ref, bf_ref, gavg_ref, lng_ref, lnb_ref, msp_ref, bsp_ref,
                      wc_ref, bc_ref, lcg_ref, lcb_ref, wd_ref, stc_ref, std_ref,
                      q_ref, k_ref, v_ref, logf_ref, vn_ref, yb_ref, yc_ref, yd_ref, nsc_ref, nsd_ref,
                      cin_s, ypre_s, din_s, ydpre_s, xs, xsd, *, nb, t):
    (q, k, v, ub, vb, ac, gc, bd, cd, hd), logf = _mixer_front(x_ref[...], wmix_ref, bf_ref)
    q_ref[...] = q * Q_SCALE
    k_ref[...] = k
    v_ref[...] = v
    logf_ref[...] = logf
    yb, vn = _gmlp(ub, vb, gavg_ref, lng_ref, lnb_ref, msp_ref, bsp_ref)
    yb_ref[...] = yb.astype(BF16)
    vn_ref[...] = vn

    cin_s[...] = ac * _sigmoid(gc)
    din_s[...] = cd * hd

    def body(bb, _):
        r0 = pl.multiple_of(bb * t, t)
        xs[0:KC - 1, :] = stc_ref[bb]
        xs[KC - 1:KC - 1 + t, :] = cin_s[pl.ds(r0, t), :]
        acc = jnp.zeros((t, W_BR), F32) + bc_ref[...]
        for j in range(KC):
            acc = acc + wc_ref[j:j + 1, :] * xs[j:j + t, :]
        ypre_s[pl.ds(r0, t), :] = acc
        nsc_ref[bb] = xs[t:t + KC - 1, :]

        xsd[0:KD - 1, :] = std_ref[bb]
        xsd[KD - 1:KD - 1 + t, :] = din_s[pl.ds(r0, t), :]
        accd = jnp.zeros((t, W_BR), F32)
        for j in range(KD):
            accd = accd + wd_ref[j:j + 1, :] * xsd[j:j + t, :]
        ydpre_s[pl.ds(r0, t), :] = accd
        nsd_ref[bb] = xsd[t:t + KD - 1, :]
        return 0

    lax.fori_loop(0, nb, body, 0)
    yc_ref[...] = _silu(_layer_norm(ypre_s[...], lcg_ref[...], lcb_ref[...])).astype(BF16)
    yd_ref[...] = (bd * ydpre_s[...]).astype(BF16)


def _k1_sample(l, x, stc, std, wp, nb, t):
    m = nb * t
    cst2 = lambda i: (l, 0, 0)
    cst4 = lambda i: (l, 0, 0, 0)
    full = lambda shp: pl.BlockSpec(shp, lambda i: (0,) * len(shp))
    in_specs = [
        full((m, D_MODEL)),
        pl.BlockSpec((None, D_MODEL, W_MIX), cst2),
        pl.BlockSpec((None, 1, LANES), cst2),
        full((W_BR, W_BR)),
        pl.BlockSpec((None, 1, W_BR), cst2),
        pl.BlockSpec((None, 1, W_BR), cst2),
        pl.BlockSpec((None, GB, m, m), cst4),
        pl.BlockSpec((None, m, W_BR), cst2),
        pl.BlockSpec((None, HALO_C, W_BR), cst2),
        pl.BlockSpec((None, 1, W_BR), cst2),
        pl.BlockSpec((None, 1, W_BR), cst2),
        pl.BlockSpec((None, 1, W_BR), cst2),
        pl.BlockSpec((None, HALO_D, W_BR), cst2),
        pl.BlockSpec((None, nb, KC - 1, W_BR), cst4),
        pl.BlockSpec((None, nb, KD - 1, W_BR), cst4),
    ]
    o = lambda dt: jax.ShapeDtypeStruct((m, W_BR), dt)
    out_shape = [o(F32), o(F32), o(F32), jax.ShapeDtypeStruct((m, LANES), F32), o(F32),
                 o(BF16), o(BF16), o(BF16),
                 jax.ShapeDtypeStruct((nb, KC - 1, W_BR), F32),
                 jax.ShapeDtypeStruct((nb, KD - 1, W_BR), F32)]
    out_specs = [full(s.shape) for s in out_shape]
    return pl.pallas_call(
        functools.partial(_k1_sample_kernel, nb=nb, t=t),
        grid=(1,), in_specs=in_specs, out_specs=out_specs, out_shape=out_shape,
        scratch_shapes=[pltpu.VMEM((m, W_BR), F32), pltpu.VMEM((m, W_BR), F32),
                        pltpu.VMEM((m, W_BR), F32), pltpu.VMEM((m, W_BR), F32),
                        pltpu.VMEM((KC - 1 + t + 2, W_BR), F32), pltpu.VMEM((2 * SUBLANES, W_BR), F32)],
        compiler_params=pltpu.CompilerParams(vmem_limit_bytes=VMEM_LIMIT), name="k1_sample",
    )(x, wp["wmix"], wp["bf"], wp["gavg"], wp["lng"], wp["lnb"], wp["msp_s"], wp["bsp_s"],
      wp["wc"], wp["bc"], wp["lcg"], wp["lcb"], wp["wd"], stc, std)


def _attn_prompt_kernel(qi_tab, ki_tab, q_ref, k_ref, v_ref, ck_ref, cq_ref, o_ref,
                        qm_s, m_s, acc_s, *, tq, tk):
    p = pl.program_id(1)
    qi = qi_tab[p]
    ki = ki_tab[p]
    last_k = ((qi + 1) * tq - 1) // tk

    @pl.when(ki == 0)
    def _():
        q = q_ref[0]
        for h in range(HA):
            q_pair = q[:, (h // 2) * LANES:(h // 2 + 1) * LANES]
            qm_s[h] = jnp.where(_lane_group_mask(tq, LANES, DH, h % 2), q_pair, jnp.zeros_like(q_pair))
        m_s[...] = jnp.full(m_s.shape, -jnp.inf, F32)
        acc_s[...] = jnp.zeros(acc_s.shape, F32)

    def step(masked, nk):
        k = k_ref[0, 0:nk, :]
        v = v_ref[0, 0:nk, :]
        ck = ck_ref[0, :, 0:nk]
        cq = cq_ref[0]
        if masked:
            keep = ((ki * tk - qi * tq) + lax.broadcasted_iota(jnp.int32, (tq, nk), 1)
                    <= lax.broadcasted_iota(jnp.int32, (tq, nk), 0))
        def scores(h):
            bias = (cq[h:h + 1, 0:1] - ck[h:h + 1, :]) * LOG2E
            return _dot_nt(qm_s[h], k[:, (h // 2) * LANES:(h // 2 + 1) * LANES]) + bias

        s_next = scores(0)
        for h in range(HA):
            s = s_next
            if h + 1 < HA:
                s_next = scores(h + 1)
            if masked:
                s = jnp.where(keep, s, -jnp.inf)
            m_prev = m_s[h]
            m_new = jnp.maximum(m_prev, jnp.max(s, axis=1, keepdims=True))
            alpha = jnp.exp2(m_prev - m_new)
            pr = jnp.exp2(s - m_new).astype(BF16)
            v_h = jnp.where(_lane_group_mask(nk, W_BR, DH, h), v, jnp.ones_like(v))
            acc_s[h] = acc_s[h] * alpha + _dot(pr, v_h)
            m_s[h] = m_new

    n_visible = (qi + 1) * tq - ki * tk
    half = tk // 2

    @pl.when(n_visible >= tk + tq)
    def _():
        step(False, tk)

    @pl.when((n_visible < tk + tq) & (n_visible > half))
    def _():
        step(True, tk)

    @pl.when(n_visible <= half)
    def _():
        step(True, half)

    @pl.when(ki == last_k)
    def _():
        out = jnp.zeros((tq, W_BR), F32)
        for h in range(HA):
            acc = acc_s[h]
            den_lane = ((h + 1) % HA) * DH
            out = out + jnp.where(_lane_group_mask(tq, W_BR, DH, h), acc / acc[:, den_lane:den_lane + 1], 0.0)
        o_ref[0] = out.astype(BF16)


def _attn_prompt(q, k, v, ct, tq, tk):
    b, s, _ = q.shape
    nq = s // tq
    n_k = [((i + 1) * tq - 1) // tk + 1 for i in range(nq)]
    qi_np = np.concatenate([np.full(n_k[i], i, np.int32) for i in range(nq)])
    ki_np = np.concatenate([np.arange(n_k[i], dtype=np.int32) for i in range(nq)])
    n_pairs = int(qi_np.shape[0])
    grid_spec = pltpu.PrefetchScalarGridSpec(
        num_scalar_prefetch=2, grid=(b, n_pairs),
        in_specs=[
            pl.BlockSpec((1, tq, W_BR), lambda bb, p, qt, kt: (bb, qt[p], 0)),
            pl.BlockSpec((1, tk, W_BR), lambda bb, p, qt, kt: (bb, kt[p], 0)),
            pl.BlockSpec((1, tk, W_BR), lambda bb, p, qt, kt: (bb, kt[p], 0)),
            pl.BlockSpec((1, SUBLANES, tk), lambda bb, p, qt, kt: (bb, 0, kt[p])),
            pl.BlockSpec((1, SUBLANES, tq), lambda bb, p, qt, kt: (bb, 0, qt[p])),
        ],
        out_specs=pl.BlockSpec((1, tq, W_BR), lambda bb, p, qt, kt: (bb, qt[p], 0)),
        scratch_shapes=[pltpu.VMEM((HA, tq, LANES), BF16), pltpu.VMEM((HA, tq, 1), F32),
                        pltpu.VMEM((HA, tq, W_BR), F32)],
    )
    return pl.pallas_call(
        functools.partial(_attn_prompt_kernel, tq=tq, tk=tk),
        grid_spec=grid_spec, out_shape=jax.ShapeDtypeStruct((b, s, W_BR), BF16),
        compiler_params=_cparams(2), name="attn_prompt",
    )(jnp.asarray(qi_np), jnp.asarray(ki_np), q, k, v, ct, ct)


def _page_cumsum_kernel(lf_ref, u_ref, o_ref):
    o_ref[...] = _dot3_r(lf_ref[...], u_ref[...])


def _page_cumsum(lf_rows, upper, pps):
    n_rows = lf_rows.shape[0]
    return pl.pallas_call(
        _page_cumsum_kernel, grid=(n_rows // (pps * SUBLANES),),
        in_specs=[pl.BlockSpec((pps * SUBLANES, PAGE), lambda i: (i, 0)),
                  _const_spec((PAGE, PAGE), lambda i: (0, 0))],
        out_specs=pl.BlockSpec((pps * SUBLANES, PAGE), lambda i: (i, 0)),
        out_shape=jax.ShapeDtypeStruct((n_rows, PAGE), F32),
        compiler_params=_cparams(1), name="page_cumsum",
    )(lf_rows, upper)


def _attn_sample_kernel(pt_ref, q_ref, kn_ref, vn_ref, lfn_ref, pmat_ref, a8_ref, ltri_ref, rsel_ref,
                        ck_hbm, cv_hbm, cc_hbm, o_ref, kbuf, vbuf, cbuf, sem, m_s, l_s, acc_s, carry_s,
                        *, l, nb, t, n_pages, pc):
    b = pl.program_id(0)
    c = pl.program_id(1)
    nc = n_pages // pc
    step_id = b * nc + c
    slot = step_id % 2
    rows = HA * t
    ng = rows // SUBLANES

    def copies(bb, cc, sl, p):
        pg = pt_ref[bb, cc * pc + p]
        return (pltpu.make_async_copy(ck_hbm.at[l, pg], kbuf.at[sl, p], sem.at[sl, 0]),
                pltpu.make_async_copy(cv_hbm.at[l, pg], vbuf.at[sl, p], sem.at[sl, 1]),
                pltpu.make_async_copy(cc_hbm.at[l, pg], cbuf.at[sl, p], sem.at[sl, 2]))

    def issue(bb, cc, sl):
        def body(p, _):
            for cp in copies(bb, cc, sl, p):
                cp.start()
            return 0
        lax.fori_loop(0, pc, body, 0)

    @pl.when(step_id == 0)
    def _():
        issue(0, 0, 0)

    @pl.when(step_id + 1 < nb * nc)
    def _():
        nxt = step_id + 1
        issue(nxt // nc, nxt % nc, 1 - slot)

    def wait_body(p, _):
        for cp in copies(b, c, slot, p):
            cp.wait()
        return 0
    lax.fori_loop(0, pc, wait_body, 0)

    @pl.when(c == 0)
    def _():
        m_s[...] = jnp.full(m_s.shape, -jnp.inf, F32)
        l_s[...] = jnp.zeros(l_s.shape, F32)
        acc_s[...] = jnp.zeros(acc_s.shape, F32)
        carry_s[...] = jnp.zeros(carry_s.shape, F32)

    q = q_ref[0]
    sub = lax.broadcasted_iota(jnp.int32, (SUBLANES, W_BR), 0)
    lane = lax.broadcasted_iota(jnp.int32, (SUBLANES, W_BR), 1)
    head_ok = (lane // DH) == (sub % HA)
    qm = []
    for g in range(t // 2):
        qa = jnp.broadcast_to(q[2 * g:2 * g + 1, :], (SUBLANES, W_BR))
        qb = jnp.broadcast_to(q[2 * g + 1:2 * g + 2, :], (SUBLANES, W_BR))
        qm.append(jnp.where(head_ok, jnp.where(sub < HA, qa, qb), 0.0))
    qm = jnp.concatenate(qm, axis=0).astype(BF16)

    def online_update(s3, pv_fn):
        m_prev = m_s[...]
        m_new = jnp.maximum(m_prev, jnp.max(s3, axis=2, keepdims=True))
        alpha = jnp.exp2(m_prev - m_new)
        pr = jnp.exp2(s3 - m_new)
        l_s[...] = alpha * l_s[...] + jnp.sum(pr, axis=2, keepdims=True)
        acc_s[...] = acc_s[...] * alpha.reshape(rows, 1) + pv_fn(pr.reshape(rows, s3.shape[2]).astype(BF16))
        m_s[...] = m_new

    c2 = cbuf[slot].reshape(pc * SUBLANES, LANES)
    off = _dot3_l(pmat_ref[...], c2)
    carry = jnp.concatenate([carry_s[:, 0:1]] * pc, axis=0)
    cf = c2 + off[:, LANES - 1:LANES] + carry
    tot = cf[(pc - 1) * SUBLANES:pc * SUBLANES, LANES - 1:LANES]
    carry_s[...] = jnp.broadcast_to(tot, carry_s.shape)
    c_past = jnp.concatenate([cf[j * SUBLANES:(j + 1) * SUBLANES, :] for j in range(pc)], axis=1)

    s_past = jnp.concatenate([_dot(qm, kbuf[slot, j].astype(BF16)) for j in range(pc)], axis=1)
    s_past = s_past.reshape(ng, SUBLANES, pc * PAGE) - (c_past * LOG2E)[None]

    def pv_past(p2):
        acc = jnp.zeros((rows, W_BR), F32)
        for j in range(pc):
            acc = acc + _dot_nt(p2[:, j * PAGE:(j + 1) * PAGE], vbuf[slot, j].astype(BF16))
        return acc

    online_update(s_past, pv_past)

    @pl.when(c == nc - 1)
    def _():
        lfp = jnp.concatenate([lfn_ref[0], jnp.zeros((LANES - t, LANES), F32)], axis=0)
        cum_new = _dot3_l(ltri_ref[...], lfp)
        c_new = _dot3_nt(a8_ref[...], cum_new) + tot
        zpad = jnp.zeros((LANES - t, W_BR), F32)
        kn = jnp.concatenate([kn_ref[0], zpad], axis=0).astype(BF16)
        vn = jnp.concatenate([vn_ref[0], zpad], axis=0).astype(BF16)
        s_new = _dot_nt(qm, kn).reshape(ng, SUBLANES, LANES) - (c_new * LOG2E)[None]
        gi = lax.broadcasted_iota(jnp.int32, s_new.shape, 0)
        si = lax.broadcasted_iota(jnp.int32, s_new.shape, 1)
        ji = lax.broadcasted_iota(jnp.int32, s_new.shape, 2)
        s_new = jnp.where(ji <= 2 * gi + si // HA, s_new, -jnp.inf)
        online_update(s_new, lambda p2: _dot(p2, vn))

        o = acc_s[...] / l_s[...].reshape(rows, 1)
        sub_r = lax.broadcasted_iota(jnp.int32, (rows, W_BR), 0)
        lane_r = lax.broadcasted_iota(jnp.int32, (rows, W_BR), 1)
        om = jnp.where((lane_r // DH) == (sub_r % HA), o, 0.0)
        o_ref[0] = _dot(rsel_ref[...], om.astype(BF16))


def _attn_sample(l, page_table, q, kn, vn, lfn, wp, ck, cv, cc, pc):
    nb, t, _ = q.shape
    n_pages = page_table.shape[1]
    rows = HA * t
    cst = lambda shp: pl.BlockSpec(shp, lambda b, c, pt: (0,) * len(shp))
    per_b = lambda w: pl.BlockSpec((1, t, w), lambda b, c, pt: (b, 0, 0))
    grid_spec = pltpu.PrefetchScalarGridSpec(
        num_scalar_prefetch=1, grid=(nb, n_pages // pc),
        in_specs=[per_b(W_BR), per_b(W_BR), per_b(W_BR), per_b(LANES),
                  cst((pc * SUBLANES, pc * SUBLANES)), cst((SUBLANES, LANES)),
                  cst((LANES, LANES)), cst((t, rows)),
                  pl.BlockSpec(memory_space=pl.ANY), pl.BlockSpec(memory_space=pl.ANY),
                  pl.BlockSpec(memory_space=pl.ANY)],
        out_specs=per_b(W_BR),
        scratch_shapes=[pltpu.VMEM((2, pc, W_BR, PAGE), F32),
                        pltpu.VMEM((2, pc, W_BR, PAGE), F32),
                        pltpu.VMEM((2, pc, SUBLANES, LANES), F32),
                        pltpu.SemaphoreType.DMA((2, 3)),
                        pltpu.VMEM((rows // SUBLANES, SUBLANES, 1), F32),
                        pltpu.VMEM((rows // SUBLANES, SUBLANES, 1), F32),
                        pltpu.VMEM((rows, W_BR), F32),
                        pltpu.VMEM((SUBLANES, LANES), F32)],
    )
    return pl.pallas_call(
        functools.partial(_attn_sample_kernel, l=l, nb=nb, t=t, n_pages=n_pages, pc=pc),
        grid_spec=grid_spec, out_shape=jax.ShapeDtypeStruct((nb, t, W_BR), F32),
        compiler_params=_cparams(2), name="attn_sample",
    )(page_table, q, kn, vn, lfn, wp["pmat"], wp["a8"], wp["ltri"], wp["rsel"], ck, cv, cc)


def _merge_kernel(x_ref, ya_ref, yb_ref, yc_ref, yd_ref, wg_ref, wbr_ref, wo_ref, g_ref, b_ref, o_ref,
                  *, alpha):
    x = x_ref[...]
    xb = x.astype(BF16)
    ys = (ya_ref, yb_ref, yc_ref, yd_ref)
    acc = jnp.zeros(x.shape, F32)
    for n in range(N_BRANCH):
        gate = _sigmoid(_dot(xb, wg_ref[:, n * D_MODEL:(n + 1) * D_MODEL]))
        acc = acc + gate * _dot(ys[n][...].astype(BF16), wbr_ref[n])
    mix = _dot(acc.astype(BF16), wo_ref[...])
    o_ref[...] = _layer_norm(alpha * x + mix, g_ref[...], b_ref[...])


def _merge(l, x, ya, yb, yc, yd, wp, tm, alpha):
    m = x.shape[0]
    row = lambda w: pl.BlockSpec((tm, w), lambda i: (i, 0))
    c3 = lambda i: (l, 0, 0)
    return pl.pallas_call(
        functools.partial(_merge_kernel, alpha=alpha),
        grid=(m // tm,),
        in_specs=[row(D_MODEL), row(W_BR), row(W_BR), row(W_BR), row(W_BR),
                  _const_spec((None, D_MODEL, N_BRANCH * D_MODEL), c3),
                  _const_spec((None, N_BRANCH, W_BR, D_MODEL), lambda i: (l, 0, 0, 0)),
                  _const_spec((None, D_MODEL, D_MODEL), c3),
                  _const_spec((None, 1, D_MODEL), c3), _const_spec((None, 1, D_MODEL), c3)],
        out_specs=row(D_MODEL), out_shape=jax.ShapeDtypeStruct((m, D_MODEL), F32),
        compiler_params=_cparams(1), name="merge",
    )(x, ya, yb, yc, yd, wp["wgate"], wp["wbr"], wp["wout"], wp["ln1g"], wp["ln1b"])


def _route_tokens_on_lanes(lt):
    sub = lax.broadcasted_iota(jnp.int32, lt.shape, 0)
    big = jnp.int32(LANES)
    lg = jnp.where(sub < N_GROUPS, lt, -jnp.inf)
    mg = jnp.max(lg, axis=0, keepdims=True)
    p_top = 1.0 / jnp.sum(jnp.exp(lg - mg), axis=0, keepdims=True)
    g_idx = jnp.min(jnp.where(lg == mg, sub, big), axis=0, keepdims=True)
    lo = N_GROUPS + g_idx * EXP_PER_GROUP
    le = jnp.where((sub >= lo) & (sub < lo + EXP_PER_GROUP), lt, -jnp.inf)
    v1 = jnp.max(le, axis=0, keepdims=True)
    i1 = jnp.min(jnp.where(le == v1, sub, big), axis=0, keepdims=True)
    le2 = jnp.where(sub == i1, -jnp.inf, le)
    v2 = jnp.max(le2, axis=0, keepdims=True)
    i2 = jnp.min(jnp.where(le2 == v2, sub, big), axis=0, keepdims=True)
    e2 = jnp.exp(v2 - v1)
    w1 = p_top / (1.0 + e2)
    w2 = p_top * e2 / (1.0 + e2)
    comb_t = jnp.where(sub == i1, w1, 0.0) + jnp.where(sub == i2, w2, 0.0)
    return g_idx, comb_t


def _moe_kernel(x_ref, ple_ref, wrh_ref, wrl_ref, br_ref, triu_ref, wg_ref, wu_ref, wd_ref, wpg_ref, wpp_ref,
                g_ref, b_ref, o_ref, moe_s, *, alpha, cap):
    x = x_ref[...]
    tm = x.shape[0]
    xb = x.astype(BF16)
    xm = (x - xb.astype(F32)).astype(BF16)
    nr = 3 * SUBLANES
    kpad = -(-cap // LANES) * LANES

    wrh = wrh_ref[...]
    lt = (_dot_nt(wrh, xb) + _dot_nt(wrl_ref[...], xb) + _dot_nt(wrh, xm) + br_ref[...])[0:nr]
    g_idx, comb_t = _route_tokens_on_lanes(lt)

    sub16 = lax.broadcasted_iota(jnp.int32, (2 * SUBLANES, tm), 0)
    onehot = jnp.where(sub16 == g_idx, 1.0, 0.0)
    before = _dot(onehot.astype(BF16), triu_ref[...])
    rank = jnp.sum(onehot * before, axis=0, keepdims=True)
    n_max = jnp.max(before[:, tm - 1:tm] + onehot[:, tm - 1:tm])

    sub8 = lax.broadcasted_iota(jnp.int32, (SUBLANES, tm), 0)
    extra = jnp.where(sub8 == 0, rank, jnp.where(sub8 == 1, g_idx.astype(F32), 0.0))
    col = jnp.concatenate([comb_t, extra, jnp.zeros((LANES - nr - SUBLANES, tm), F32)], axis=0).T
    rank_c = col[:, nr:nr + 1].astype(jnp.int32)
    g_c = col[:, nr + 1:nr + 2].astype(jnp.int32)

    def experts(rows_b, comb_rows, e_list):
        y = jnp.zeros((rows_b.shape[0], D_MODEL), F32)
        e_list = list(e_list)
        up = lambda e: (_dot(rows_b, wg_ref[e]), _dot(rows_b, wu_ref[e]))
        nxt = up(e_list[0])
        held = None
        for i, e in enumerate(e_list):
            hg, hu = nxt
            if i + 1 < len(e_list):
                nxt = up(e_list[i + 1])
            hh = (_silu(hg) * hu * comb_rows[:, N_GROUPS + e:N_GROUPS + e + 1]).astype(BF16)
            if held is not None:
                y = y + _dot(held[0], wd_ref[held[1]])
            held = (hh, e)
        return y + _dot(held[0], wd_ref[held[1]])

    @pl.when(n_max <= cap)
    def _():
        comb_hi = col.astype(BF16)
        comb_lo = (col - comb_hi.astype(F32)).astype(BF16)
        r_iota = lax.broadcasted_iota(jnp.int32, (cap, tm), 0)
        c_iota = lax.broadcasted_iota(jnp.int32, (tm, kpad), 1)
        rank_i = rank.astype(jnp.int32)
        acc = jnp.zeros((tm, D_MODEL), F32)
        for g in range(N_GROUPS):
            pg = jnp.where((r_iota == rank_i) & (g_idx == g), 1.0, 0.0).astype(BF16)
            pgt = jnp.where((c_iota == rank_c) & (g_c == g), 1.0, 0.0).astype(BF16)
            xs = _dot(pg, xb).astype(BF16)
            cs = _dot(pg, comb_hi) + _dot(pg, comb_lo)
            y = experts(xs, cs, range(g * EXP_PER_GROUP, (g + 1) * EXP_PER_GROUP))
            if kpad > cap:
                y = jnp.concatenate([y, jnp.zeros((kpad - cap, D_MODEL), F32)], axis=0)
            acc = acc + _dot(pgt, y.astype(BF16))
        moe_s[...] = acc

    @pl.when(n_max > cap)
    def _():
        moe_s[...] = experts(xb, col, range(N_EXPERTS))

    ple_term = _sigmoid(_dot(xb, wpg_ref[...])) * _dot(ple_ref[...].astype(BF16), wpp_ref[...])
    o_ref[...] = _layer_norm(alpha * x + moe_s[...] + ple_term, g_ref[...], b_ref[...])


def _moe(l, x, ple, wp, tm, alpha, triu):
    m = x.shape[0]
    cap = (tm * 5) // 16
    c3 = lambda i: (l, 0, 0)
    c4 = lambda i: (l, 0, 0, 0)
    return pl.pallas_call(
        functools.partial(_moe_kernel, alpha=alpha, cap=cap),
        grid=(m // tm,),
        in_specs=[pl.BlockSpec((tm, D_MODEL), lambda i: (i, 0)),
                  pl.BlockSpec((None, tm, D_PLE), lambda i: (l, i, 0)),
                  _const_spec((None, LANES, D_MODEL), c3), _const_spec((None, LANES, D_MODEL), c3),
                  _const_spec((None, LANES, 1), c3),
                  _const_spec((tm, tm), lambda i: (0, 0)),
                  _const_spec((None, N_EXPERTS, D_MODEL, D_EXPERT), c4),
                  _const_spec((None, N_EXPERTS, D_MODEL, D_EXPERT), c4),
                  _const_spec((None, N_EXPERTS, D_EXPERT, D_MODEL), c4),
                  _const_spec((None, D_MODEL, D_MODEL), c3), _const_spec((None, D_PLE, D_MODEL), c3),
                  _const_spec((None, 1, D_MODEL), c3), _const_spec((None, 1, D_MODEL), c3)],
        out_specs=pl.BlockSpec((tm, D_MODEL), lambda i: (i, 0)),
        out_shape=jax.ShapeDtypeStruct((m, D_MODEL), F32),
        scratch_shapes=[pltpu.VMEM((tm, D_MODEL), F32)],
        compiler_params=_cparams(1), name="moe",
    )(x, ple, wp["wr_hi"], wp["wr_lo"], wp["br"], triu, wp["weg"], wp["weu"], wp["wed"],
      wp["wpg"], wp["wpp"], wp["ln2g"], wp["ln2b"])


def _kron_tril(w_spatial, n_blocks, n):
    w = jnp.tril(w_spatial)[:, :, :n, :n]
    eye = jnp.eye(n_blocks, dtype=w.dtype)
    out = jnp.einsum("ab,lgij->lgaibj", eye, w)
    d = w_spatial.shape[0]
    return out.reshape(d, GB, n_blocks * n, n_blocks * n).astype(BF16)


def _prepare(w_in, b_forget, ln_gmlp_g, ln_gmlp_b, w_spatial, b_spatial, w_conv_c, b_conv_c, ln_conv_g,
             ln_conv_b, w_conv_d, w_branch, w_out, ln1_g, ln1_b, w_router_group, b_router_group,
             w_router_expert, b_router_expert, w_exp_gate, w_exp_up, w_exp_down, w_ple_gate, w_ple_proj,
             ln2_g, ln2_b, tm_p, nb, t, pc, pps):
    d = w_in.shape[0]
    o_f = 3 * W_BR
    o_rest = o_f + HA
    o_gate = o_rest + 7 * W_BR
    r3 = lambda a: a.reshape(d, 1, -1)
    wp = {}
    wp["wmix"] = jnp.concatenate(
        [w_in[:, :, 0:o_f], w_in[:, :, o_rest:o_gate],
         jnp.pad(w_in[:, :, o_f:o_rest], ((0, 0), (0, 0), (0, LANES - HA)))], axis=-1).astype(BF16)
    wp["wgate"] = w_in[:, :, o_gate:].astype(BF16)
    wp["bf"] = jnp.pad(b_forget, ((0, 0), (0, LANES - HA))).reshape(d, 1, LANES)
    gi = np.arange(W_BR) // DB
    wp["gavg"] = jnp.asarray((gi[:, None] == gi[None, :]).astype(np.float32) / DB, dtype=BF16)
    wp["lng"] = r3(ln_gmlp_g)
    wp["lnb"] = r3(ln_gmlp_b)
    wp["msp_p"] = _kron_tril(w_spatial, tm_p // CHUNK, CHUNK)
    wp["msp_s"] = _kron_tril(w_spatial, nb, t)
    bsp = jnp.repeat(jnp.swapaxes(b_spatial, 1, 2), DB, axis=2)
    wp["bsp_p"] = jnp.tile(bsp, (1, tm_p // CHUNK, 1))
    wp["bsp_s"] = jnp.tile(bsp[:, :t, :], (1, nb, 1))
    wp["wc"] = jnp.pad(w_conv_c, ((0, 0), (0, HALO_C - KC), (0, 0)))
    wp["bc"] = r3(b_conv_c)
    wp["lcg"] = r3(ln_conv_g)
    wp["lcb"] = r3(ln_conv_b)
    wp["wd"] = jnp.pad(w_conv_d, ((0, 0), (0, HALO_D - KD), (0, 0)))
    wp["tril_p"] = jnp.asarray(np.tril(np.ones((tm_p, tm_p), np.float32)), dtype=BF16)
    wp["wbr"] = w_branch.astype(BF16)
    wp["wout"] = w_out.astype(BF16)
    wp["ln1g"] = r3(ln1_g)
    wp["ln1b"] = r3(ln1_b)
    wr = jnp.swapaxes(jnp.concatenate([w_router_group, w_router_expert,
                                       jnp.zeros((d, D_MODEL, LANES - N_GROUPS - N_EXPERTS), F32)], axis=-1), 1, 2)
    wr_hi = wr.astype(BF16)
    wp["wr_hi"] = wr_hi
    wp["wr_lo"] = (wr - wr_hi.astype(F32)).astype(BF16)
    wp["br"] = jnp.concatenate([b_router_group, b_router_expert,
                                jnp.zeros((d, LANES - N_GROUPS - N_EXPERTS), F32)], axis=-1).reshape(d, LANES, 1)
    strict_upper = lambda n: jnp.asarray(np.triu(np.ones((n, n), np.float32), 1), dtype=BF16)
    wp["triu_p"] = strict_upper(tm_p)
    wp["triu_s"] = strict_upper(nb * t)
    wp["weg"] = w_exp_gate.astype(BF16)
    wp["weu"] = w_exp_up.astype(BF16)
    wp["wed"] = w_exp_down.astype(BF16)
    wp["wpg"] = w_ple_gate.astype(BF16)
    wp["wpp"] = w_ple_proj.astype(BF16)
    wp["ln2g"] = r3(ln2_g)
    wp["ln2b"] = r3(ln2_b)

    rows = pc * SUBLANES
    ri = np.arange(rows)
    pm = ((ri[:, None] % SUBLANES) == (ri[None, :] % SUBLANES)) & ((ri[None, :] // SUBLANES) < (ri[:, None] // SUBLANES))
    wp["pmat"] = jnp.asarray(pm.astype(np.float32), dtype=BF16)
    a8 = (np.arange(LANES)[None, :] == (np.arange(SUBLANES)[:, None] % HA)).astype(np.float32)
    wp["a8"] = jnp.asarray(a8, dtype=BF16)
    wp["ltri"] = jnp.asarray(np.tril(np.ones((LANES, LANES), np.float32)), dtype=BF16)
    rsel = ((np.arange(HA * t)[None, :] // HA) == np.arange(t)[:, None]).astype(np.float32)
    wp["rsel"] = jnp.asarray(rsel, dtype=BF16)
    wp["upper"] = jnp.asarray(np.triu(np.ones((PAGE, PAGE), np.float32)), dtype=BF16)
    return wp


def kernel(x_prompt, x_sample, cache_k, cache_v, cache_logf, state_conv_c, state_conv_d, page_table,
           p_prompt, p_sample, w_in, b_forget, ln_gmlp_g, ln_gmlp_b, w_spatial, b_spatial, w_conv_c,
           b_conv_c, ln_conv_g, ln_conv_b, w_conv_d, w_branch, w_out, ln1_g, ln1_b, w_router_group,
           b_router_group, w_router_expert, b_router_expert, w_exp_gate, w_exp_up, w_exp_down,
           w_ple_gate, w_ple_proj, ln2_g, ln2_b):
    depth = w_in.shape[0]
    alpha = float((2 * depth) ** 0.25)
    b, s, _ = x_prompt.shape
    nb, t, _ = x_sample.shape
    n_pool = cache_k.shape[1]
    n_pages = page_table.shape[1]
    tm_p = min(512, s)
    tq, tk = min(512, s), min(1024, s)
    ms = nb * t
    pc = min(64, n_pages)
    n_pg = depth * n_pool
    pps = next(c for c in (256, 128, 64, 32, 16, 8, 4, 2, 1) if n_pg % c == 0)

    wp = _prepare(w_in, b_forget, ln_gmlp_g, ln_gmlp_b, w_spatial, b_spatial, w_conv_c, b_conv_c,
                  ln_conv_g, ln_conv_b, w_conv_d, w_branch, w_out, ln1_g, ln1_b, w_router_group,
                  b_router_group, w_router_expert, b_router_expert, w_exp_gate, w_exp_up, w_exp_down,
                  w_ple_gate, w_ple_proj, ln2_g, ln2_b, tm_p, nb, t, pc, pps)

    ck = jnp.transpose(cache_k, (0, 1, 3, 4, 2)).reshape(depth, n_pool, W_BR, PAGE)
    cv = jnp.transpose(cache_v, (0, 1, 3, 4, 2)).reshape(depth, n_pool, W_BR, PAGE)
    lf_t = jnp.transpose(cache_logf, (0, 1, 3, 2))
    lf_rows = jnp.concatenate([lf_t, lf_t], axis=2).reshape(n_pg * SUBLANES, PAGE)
    cc = _page_cumsum(lf_rows, wp["upper"], pps).reshape(depth, n_pool, SUBLANES, LANES)

    pp = p_prompt.reshape(depth, b * s, D_PLE)
    ps = p_sample.reshape(depth, ms, D_PLE)

    xp = x_prompt
    xs = x_sample.reshape(ms, D_MODEL)
    kp, vp, lfp, ccp, cdp = [], [], [], [], []
    ks_, vs_, lfs, ccs, cds, gvs = [], [], [], [], [], []
    for l in range(depth):
        (q, k32, v32, kb, vb, logf, ct, yb, yc, yd, stc, std) = _k1_prompt(l, xp, wp, tm_p)
        ya = _attn_prompt(q, kb, vb, ct, tq, tk)
        f2 = lambda a: a.reshape(b * s, a.shape[-1])
        x1 = _merge(l, f2(xp), f2(ya), f2(yb), f2(yc), f2(yd), wp, tm_p, alpha)
        xp = _moe(l, x1, pp, wp, tm_p, alpha, wp["triu_p"]).reshape(b, s, D_MODEL)
        kp.append(k32.reshape(b, s, HA, DH))
        vp.append(v32.reshape(b, s, HA, DH))
        lfp.append(logf)
        ccp.append(stc[:, HALO_C - (KC - 1):, :])
        cdp.append(std[:, HALO_D - (KD - 1):, :])

        (qs, k_s, v_s, lf_s, vn_s, yb_s, yc_s, yd_s, nsc, nsd) = _k1_sample(
            l, xs, state_conv_c, state_conv_d, wp, nb, t)
        r3 = lambda a: a.reshape(nb, t, a.shape[-1])
        ya_s = _attn_sample(l, page_table, r3(qs), r3(k_s), r3(v_s), r3(lf_s), wp, ck, cv, cc, pc)
        x1s = _merge(l, xs, ya_s.reshape(ms, W_BR), yb_s, yc_s, yd_s, wp, ms, alpha)
        xs = _moe(l, x1s, ps, wp, ms, alpha, wp["triu_s"])
        ks_.append(k_s.reshape(nb, t, HA, DH))
        vs_.append(v_s.reshape(nb, t, HA, DH))
        lfs.append(lf_s[:, 0:HA].reshape(nb, t, HA))
        ccs.append(nsc)
        cds.append(nsd)
        gvs.append(vn_s.reshape(nb, t, GB, DB))

    return (xp, xs.reshape(nb, t, D_MODEL), jnp.stack(kp), jnp.stack(vp), jnp.stack(lfp),
            jnp.stack(ks_), jnp.stack(vs_), jnp.stack(lfs), jnp.stack(ccp), jnp.stack(ccs),
            jnp.stack(cdp), jnp.stack(cds), jnp.stack(gvs))
```
